```python
import math
import numpy as np
import jax
import jax.numpy as jnp
from jax import lax

D_MODEL = 2048
BATCH = 4
SEQ = 4096
DEPTH = 2

D_MIX = D_MODEL
D_GDN = D_MIX // 2
D_DIFF = D_MIX - D_GDN
GDN_HEAD_DIM = 128
GDN_HEADS = D_GDN // GDN_HEAD_DIM
GDN_CONV = 5
GDN_CHUNK = 64
N_DIR = 2
DIFF_QK_DIM = 128
DIFF_V_DIM = 2 * DIFF_QK_DIM
DIFF_HEADS = D_DIFF // DIFF_V_DIM
Q_BLOCK = 128
ROPE_THETA = 10000.0
N_EXPERTS = 16
EC_CAPACITY = 2
D_EXPERT = D_MODEL
NORM_EPS = 1e-6
IN_SIZES = (D_GDN, D_GDN, D_GDN, D_GDN, N_DIR * GDN_HEADS, N_DIR * GDN_HEADS, D_DIFF, D_DIFF, D_DIFF)
N_IN = sum(IN_SIZES)
IN_SPLITS = tuple(int(v) for v in np.cumsum(IN_SIZES)[:-1])

kernel_name = 'hybrid_gdn_diffattn_ec_moe_encoder'

F32 = jnp.float32


def rms_norm(x, w):
    xf = x.astype(F32)
    y = xf * lax.rsqrt(jnp.mean(xf * xf, axis=-1, keepdims=True) + NORM_EPS)
    return (y * w.astype(F32)).astype(x.dtype)


def l2norm(t):
    tf = t.astype(F32)
    return tf * lax.rsqrt(jnp.sum(tf * tf, axis=-1, keepdims=True) + NORM_EPS)


def rope_tables(s):
    half = DIFF_QK_DIM // 2
    inv_freq = ROPE_THETA ** (-jnp.arange(half, dtype=F32) / half)
    ang = jnp.arange(s, dtype=F32)[:, None] * inv_freq[None, :]
    return jnp.cos(ang), jnp.sin(ang)


def rotary(t, cos, sin):
    half = t.shape[-1] // 2
    tf = t.astype(F32)
    t1, t2 = tf[..., :half], tf[..., half:]
    c, s_ = cos[None, :, None, :], sin[None, :, None, :]
    return jnp.concatenate([t1 * c - t2 * s_, t2 * c + t1 * s_], axis=-1).astype(t.dtype)


def centred_depthwise_conv(x, w):
    k, c = w.shape
    return lax.conv_general_dilated(
        x, w.reshape(k, 1, c).astype(x.dtype), window_strides=(1,),
        padding=[(k // 2, k // 2)], dimension_numbers=('NWC', 'WIO', 'NWC'),
        feature_group_count=c)


def gated_delta_chunked(q, k, v, beta, g):
    b, h, s, dk = q.shape
    dv = v.shape[-1]
    c = GDN_CHUNK
    n = s // c
    q, k, v = (t.astype(F32).reshape(b, h, n, c, -1) for t in (q, k, v))
    beta = beta.astype(F32).reshape(b, h, n, c)
    gc = jnp.cumsum(g.astype(F32).reshape(b, h, n, c), axis=-1)
    incl_lower = jnp.tril(jnp.ones((c, c), bool))
    strict_lower = jnp.tril(jnp.ones((c, c), bool), -1)
    gdiff = gc[..., :, None] - gc[..., None, :]
    decay = jnp.where(incl_lower, jnp.exp(jnp.where(incl_lower, gdiff, 0.0)), 0.0)
    kb = k * beta[..., None]
    lower = jnp.where(strict_lower, jnp.einsum('bhncd,bhnsd->bhncs', kb, k) * decay, 0.0)
    eye = jnp.broadcast_to(jnp.eye(c, dtype=F32), lower.shape)
    rhs = jnp.concatenate([v * beta[..., None], kb * jnp.exp(gc)[..., None]], axis=-1)
    sol = lax.linalg.triangular_solve(eye + lower, rhs, left_side=True, lower=True, unit_diagonal=True)
    u, w = sol[..., :dv], sol[..., dv:]
    qk = jnp.einsum('bhncd,bhnsd->bhncs', q, k) * decay
    qg = q * jnp.exp(gc)[..., None]
    kd = k * jnp.exp(gc[..., -1:] - gc)[..., None]
    g_last = jnp.exp(gc[..., -1])

    def step(state, xs):
        u_i, w_i, qk_i, qg_i, kd_i, gl_i = xs
        v_new = u_i - jnp.einsum('bhcd,bhde->bhce', w_i, state)
        o_i = jnp.einsum('bhcd,bhde->bhce', qg_i, state) + jnp.einsum('bhcs,bhse->bhce', qk_i, v_new)
        state = state * gl_i[..., None, None] + jnp.einsum('bhcd,bhce->bhde', kd_i, v_new)
        return state, o_i

    xs = tuple(jnp.moveaxis(t, 2, 0) for t in (u, w, qk, qg, kd, g_last))
    _, o = lax.scan(step, jnp.zeros((b, h, dk, dv), F32), xs)
    return jnp.moveaxis(o, 0, 2).reshape(b, h, s, dv)


def gdn_mixer(q, k, v, z, b_gate, a_gate, conv_w, a_log, dt_bias, norm_w):
    bsz, s, _ = q.shape
    qkv = jax.nn.silu(centred_depthwise_conv(jnp.concatenate([q, k, v], axis=-1), conv_w))
    q, k, v = jnp.split(qkv, 3, axis=-1)
    heads = lambda t: t.reshape(bsz, s, GDN_HEADS, GDN_HEAD_DIM).transpose(0, 2, 1, 3)
    q = l2norm(heads(q)) * (GDN_HEAD_DIM ** -0.5)
    k = l2norm(heads(k))
    v = heads(v)
    per_dir = lambda t: t.astype(F32).reshape(bsz, s, N_DIR, GDN_HEADS).transpose(2, 0, 3, 1)
    beta = jax.nn.sigmoid(per_dir(b_gate))
    g = -jnp.exp(a_log.astype(F32))[:, None, :, None] * jax.nn.softplus(
        per_dir(a_gate) + dt_bias.astype(F32)[:, None, :, None])
    o_fwd = gated_delta_chunked(q, k, v, beta[0], g[0])
    flip = lambda t: jnp.flip(t, axis=2)
    o_bwd = flip(gated_delta_chunked(flip(q), flip(k), flip(v), flip(beta[1]), flip(g[1])))
    o = (o_fwd + o_bwd).transpose(0, 2, 1, 3)
    zh = z.astype(F32).reshape(bsz, s, GDN_HEADS, GDN_HEAD_DIM)
    o = rms_norm(o, norm_w) * jax.nn.silu(zh)
    return o.reshape(bsz, s, D_GDN).astype(z.dtype)


def diff_attention(q, k, v, lam_vecs, subln_w, lambda_init, cos, sin):
    bsz, s, _ = q.shape

    def qk_heads(t):
        t = rotary(t.reshape(bsz, s, 2 * DIFF_HEADS, DIFF_QK_DIM), cos, sin)
        return t.reshape(bsz, s, DIFF_HEADS, 2, DIFF_QK_DIM).transpose(0, 2, 3, 1, 4)

    q = qk_heads(q) * (DIFF_QK_DIM ** -0.5)
    k = qk_heads(k)
    v = v.reshape(bsz, s, DIFF_HEADS, DIFF_V_DIM).transpose(0, 2, 1, 3)
    lf = lam_vecs.astype(F32)
    lam = jnp.exp(jnp.sum(lf[0] * lf[1])) - jnp.exp(jnp.sum(lf[2] * lf[3])) + lambda_init
    nb = s // Q_BLOCK
    qb = jnp.moveaxis(q.reshape(bsz, DIFF_HEADS, 2, nb, Q_BLOCK, DIFF_QK_DIM), 3, 0)

    def block(q_blk):
        scores = jnp.einsum('bhtqd,bhtkd->bhtqk', q_blk, k, preferred_element_type=F32)
        p = jax.nn.softmax(scores, axis=-1)
        wts = (p[:, :, 0] - lam * p[:, :, 1]).astype(v.dtype)
        return jnp.einsum('bhqk,bhkd->bhqd', wts, v)

    o = lax.map(block, qb)
    o = jnp.moveaxis(o, 0, 2).reshape(bsz, DIFF_HEADS, s, DIFF_V_DIM)
    o = rms_norm(o, subln_w) * (1.0 - lambda_init)
    return o.transpose(0, 2, 1, 3).reshape(bsz, s, D_DIFF)


def expert_choice_ffn(x, w_router, w_gate, w_up, w_down):
    bsz, s, d = x.shape
    cap = EC_CAPACITY * s // N_EXPERTS
    logits = jnp.einsum('bsd,de->bse', x, w_router, preferred_element_type=F32)
    aff = jax.nn.softmax(logits, axis=-1)
    gates, idx = lax.top_k(jnp.swapaxes(aff, 1, 2), cap)
    xs = jax.vmap(lambda xb, ib: xb[ib])(x, idx)
    hid = jax.nn.silu(jnp.einsum('becd,edf->becf', xs, w_gate)) * jnp.einsum('becd,edf->becf', xs, w_up)
    y = jnp.einsum('becf,efd->becd', hid, w_down) * gates[..., None].astype(x.dtype)
    return jax.vmap(lambda ib, yb: jnp.zeros((s, d), yb.dtype).at[ib.reshape(-1)].add(yb.reshape(-1, d)))(idx, y)


def setup_inputs(seed: int = 0) -> dict:
    key = jax.random.key(seed)
    ks = jax.random.split(key, 17)
    nrm = lambda k, shape, scale: jax.random.normal(k, shape, F32) * scale
    x = nrm(ks[0], (BATCH, SEQ, D_MODEL), 1.0)
    norm1_w = 1.0 + nrm(ks[1], (DEPTH, D_MODEL), 0.02)
    w_in = nrm(ks[2], (DEPTH, D_MODEL, N_IN), D_MODEL ** -0.5)
    conv_w = nrm(ks[3], (DEPTH, GDN_CONV, 3 * D_GDN), GDN_CONV ** -0.5)
    a_log = jnp.log(jax.random.uniform(ks[4], (DEPTH, N_DIR, GDN_HEADS), F32, 1.0, 16.0))
    dt = jnp.exp(jax.random.uniform(ks[5], (DEPTH, N_DIR, GDN_HEADS), F32, math.log(1e-3), math.log(1e-1)))
    dt_bias = dt + jnp.log(-jnp.expm1(-dt))
    gdn_norm_w = 1.0 + nrm(ks[6], (DEPTH, GDN_HEAD_DIM), 0.02)
    diff_lambda = nrm(ks[7], (DEPTH, 4, DIFF_QK_DIM), 0.1)
    diff_subln_w = 1.0 + nrm(ks[8], (DEPTH, DIFF_V_DIM), 0.02)
    w_out = nrm(ks[9], (DEPTH, D_MIX, D_MODEL), D_MIX ** -0.5)
    norm2_w = 1.0 + nrm(ks[10], (DEPTH, D_MODEL), 0.02)
    w_router = nrm(ks[11], (DEPTH, D_MODEL, N_EXPERTS), D_MODEL ** -0.5)
    w_gate = nrm(ks[12], (DEPTH, N_EXPERTS, D_MODEL, D_EXPERT), D_MODEL ** -0.5)
    w_up = nrm(ks[13], (DEPTH, N_EXPERTS, D_MODEL, D_EXPERT), D_MODEL ** -0.5)
    w_down = nrm(ks[14], (DEPTH, N_EXPERTS, D_EXPERT, D_MODEL), D_EXPERT ** -0.5)
    final_norm_w = 1.0 + nrm(ks[15], (D_MODEL,), 0.02)
    return {'x': x, 'norm1_w': norm1_w, 'w_in': w_in, 'conv_w': conv_w, 'a_log': a_log,
            'dt_bias': dt_bias, 'gdn_norm_w': gdn_norm_w, 'diff_lambda': diff_lambda,
            'diff_subln_w': diff_subln_w, 'w_out': w_out, 'norm2_w': norm2_w,
            'w_router': w_router, 'w_gate': w_gate, 'w_up': w_up, 'w_down': w_down,
            'final_norm_w': final_norm_w}


def reference(x, norm1_w, w_in, conv_w, a_log, dt_bias, gdn_norm_w, diff_lambda,
              diff_subln_w, w_out, norm2_w, w_router, w_gate, w_up, w_down, final_norm_w):
    cos, sin = rope_tables(x.shape[1])
    for l in range(DEPTH):
        n = rms_norm(x, norm1_w[l])
        proj = jnp.einsum('bsd,dn->bsn', n, w_in[l])
        qa, ka, va, za, ba, aa, qb, kb, vb = jnp.split(proj, IN_SPLITS, axis=-1)
        out_a = gdn_mixer(qa, ka, va, za, ba, aa, conv_w[l], a_log[l], dt_bias[l], gdn_norm_w[l])
        lambda_init = 0.8 - 0.6 * math.exp(-0.3 * l)
        out_b = diff_attention(qb, kb, vb, diff_lambda[l], diff_subln_w[l], lambda_init, cos, sin)
        mixed = jnp.concatenate([out_a, out_b], axis=-1)
        x = x + jnp.einsum('bsm,md->bsd', mixed, w_out[l])
        x = x + expert_choice_ffn(rms_norm(x, norm2_w[l]), w_router[l], w_gate[l], w_up[l], w_down[l])
    return rms_norm(x, final_norm_w)
```

```python
import functools
import math

import jax
import jax.numpy as jnp
from jax import lax
from jax.experimental import pallas as pl
from jax.experimental.pallas import tpu as pltpu

F32 = jnp.float32
BF16 = jnp.bfloat16
I32 = jnp.int32

NORM_EPS = 1e-6
LANES = 128
HEAD = 128
GDN_CONV = 5
CHUNK = 64
N_DIR = 2
N_EXPERTS = 16
EC_CAPACITY = 2
ROPE_THETA = 10000.0
VMEM_LIMIT = 56 * 1024 * 1024
HI = lax.Precision.HIGHEST
SCAN_ROWS = 128
ATTN_TQ = 256


def _cparams(sem):
    return pltpu.CompilerParams(dimension_semantics=sem, vmem_limit_bytes=VMEM_LIMIT)


def _sigmoid(x):
    return 1.0 / (1.0 + jnp.exp(-x))


def _mm(a, b):
    return jnp.dot(a.astype(BF16), b.astype(BF16), preferred_element_type=F32)


def _mm_nt(a, b):
    return lax.dot_general(a.astype(BF16), b.astype(BF16), (((1,), (1,)), ((), ())),
                           preferred_element_type=F32)


def _mm_tn(a, b):
    return lax.dot_general(a.astype(BF16), b.astype(BF16), (((0,), (0,)), ((), ())),
                           preferred_element_type=F32)


def _norm_rows(x_ref, nw_ref, n_scr):
    x = x_ref[...]
    ms = jnp.mean(x * x, axis=-1, keepdims=True)
    n_scr[...] = (x * lax.rsqrt(ms + NORM_EPS) * nw_ref[...]).astype(BF16)


def _norm_proj_kernel(x_ref, nw_ref, w_ref, o_ref, n_scr):
    @pl.when(pl.program_id(1) == 0)
    def _():
        _norm_rows(x_ref, nw_ref, n_scr)

    o_ref[...] = jnp.dot(n_scr[...], w_ref[...], preferred_element_type=F32).astype(o_ref.dtype)


def _norm_proj_rot_kernel(x_ref, nw_ref, w_ref, cos_ref, sin_ref, o_ref, n_scr, *,
                          n_q_blocks, n_rot_blocks):
    j = pl.program_id(1)

    @pl.when(j == 0)
    def _():
        _norm_rows(x_ref, nw_ref, n_scr)

    acc = jnp.dot(n_scr[...], w_ref[...], preferred_element_type=F32)
    tn = acc.shape[1]

    @pl.when(j < n_rot_blocks)
    def _():
        scale = jnp.where(j < n_q_blocks, HEAD ** -0.5, 1.0).astype(F32)
        c = cos_ref[...]
        s = sin_ref[...]
        for g in range(tn // HEAD):
            t = acc[:, g * HEAD:(g + 1) * HEAD]
            r = t * c + pltpu.roll(t, HEAD // 2, axis=1) * s
            o_ref[:, g * HEAD:(g + 1) * HEAD] = (r * scale).astype(o_ref.dtype)

    @pl.when(j >= n_rot_blocks)
    def _():
        o_ref[...] = acc.astype(o_ref.dtype)


def _norm_proj(x2d, nw, w_bf16, *, tm, tn, out_dtype, rot=None):
    m, d = x2d.shape
    n = w_bf16.shape[1]
    grid = (m // tm, n // tn)
    in_specs = [pl.BlockSpec((tm, d), lambda i, j: (i, 0)),
                pl.BlockSpec((1, d), lambda i, j: (0, 0)),
                pl.BlockSpec((d, tn), lambda i, j: (0, j))]
    args = [x2d, nw.reshape(1, d), w_bf16]
    if rot is None:
        kern = _norm_proj_kernel
    else:
        cos_t, sin_t, seq, n_q_blocks, n_rot_blocks = rot
        nsb = seq // tm
        in_specs += [pl.BlockSpec((tm, HEAD), lambda i, j: (i % nsb, 0)),
                     pl.BlockSpec((tm, HEAD), lambda i, j: (i % nsb, 0))]
        args += [cos_t, sin_t]
        kern = functools.partial(_norm_proj_rot_kernel, n_q_blocks=n_q_blocks,
                                 n_rot_blocks=n_rot_blocks)
    return pl.pallas_call(
        kern,
        out_shape=jax.ShapeDtypeStruct((m, n), out_dtype),
        grid=grid,
        in_specs=in_specs,
        out_specs=pl.BlockSpec((tm, tn), lambda i, j: (i, j)),
        scratch_shapes=[pltpu.VMEM((tm, d), BF16)],
        compiler_params=_cparams(("parallel", "arbitrary")),
        name="norm_proj" if rot is None else "norm_proj_rot",
    )(*args)


def _gdn_prep_kernel(x_ref, w_ref, o_ref, *, n_heads):
    j = pl.program_id(1)
    x = x_ref[0]
    w = w_ref[...]
    s = x.shape[0]
    row = lax.broadcasted_iota(I32, x.shape, 0)
    half = GDN_CONV // 2
    acc = x * w[half:half + 1, :]
    for k in range(GDN_CONV):
        off = k - half
        if off == 0:
            continue
        xs = pltpu.roll(x, (-off) % s, axis=0)
        valid = jnp.logical_and(row + off >= 0, row + off < s)
        acc = acc + jnp.where(valid, xs, 0.0) * w[k:k + 1, :]
    y = acc * _sigmoid(acc)
    ss = jnp.sum(y * y, axis=-1, keepdims=True)
    nrm = y * lax.rsqrt(ss + NORM_EPS)
    out = jnp.where(j < n_heads, nrm * (HEAD ** -0.5), jnp.where(j < 2 * n_heads, nrm, y))
    o_ref[0] = out


def _gdn_prep(proj3d, conv_w, n_heads):
    b, s, _ = proj3d.shape
    ncol = 3 * n_heads
    return pl.pallas_call(
        functools.partial(_gdn_prep_kernel, n_heads=n_heads),
        out_shape=jax.ShapeDtypeStruct((b, s, ncol * HEAD), F32),
        grid=(b, ncol),
        in_specs=[pl.BlockSpec((1, s, HEAD), lambda i, j: (i, 0, j)),
                  pl.BlockSpec((GDN_CONV, HEAD), lambda i, j: (0, j))],
        out_specs=pl.BlockSpec((1, s, HEAD), lambda i, j: (i, 0, j)),
        compiler_params=_cparams(("parallel", "parallel")),
        name="gdn_prep",
    )(proj3d, conv_w)


def _gate_prep_kernel(x_ref, alog_ref, dtb_ref, col_ref, row_ref, *, n_gate):
    x = x_ref[...]
    lane = lax.broadcasted_iota(I32, x.shape, 1)
    beta = _sigmoid(x)
    a = x + dtb_ref[...]
    sp = jnp.maximum(a, 0.0) + jnp.log(1.0 + jnp.exp(-jnp.abs(a)))
    g = -jnp.exp(alog_ref[...]) * sp
    y = jnp.where(lane < n_gate, beta, jnp.where(lane < 2 * n_gate, g, 0.0))
    col_ref[...] = y
    row_ref[0] = y.T


def _gate_prep(gates2d, a_log, dt_bias, *, seq, tm):
    m = gates2d.shape[0]
    n_gate = a_log.size
    nsb = seq // tm
    pad = lambda v: jnp.zeros((1, LANES), F32).at[0, n_gate:2 * n_gate].set(v.reshape(-1).astype(F32))
    return pl.pallas_call(
        functools.partial(_gate_prep_kernel, n_gate=n_gate),
        out_shape=(jax.ShapeDtypeStruct((m, LANES), F32),
                   jax.ShapeDtypeStruct((m // seq, LANES, seq), F32)),
        grid=(m // tm,),
        in_specs=[pl.BlockSpec((tm, LANES), lambda i: (i, 0)),
                  pl.BlockSpec((1, LANES), lambda i: (0, 0)),
                  pl.BlockSpec((1, LANES), lambda i: (0, 0))],
        out_specs=(pl.BlockSpec((tm, LANES), lambda i: (i, 0)),
                   pl.BlockSpec((1, LANES, tm), lambda i: (i // nsb, 0, i % nsb))),
        compiler_params=_cparams(("parallel",)),
        name="gate_prep",
    )(gates2d, pad(a_log), pad(dt_bias))


def _gdn_chunk(q, k, v, beta_c, gc_c, gc_r, gtot, state, *, reverse, tri):
    strict, incl, eye = tri[reverse]
    gdiff = gc_c - gc_r
    decay = jnp.where(incl, jnp.exp(jnp.where(incl, gdiff, 0.0)), 0.0)
    kb = k * beta_c
    aq = _mm_nt(jnp.concatenate([kb, q], axis=0), k)
    c = q.shape[0]
    low = jnp.where(strict, aq[:c] * decay, 0.0)
    qk = aq[c:] * decay
    inv = eye - low
    p = low
    for _ in range(int(math.log2(c)) - 1):
        p = _mm(p, p)
        inv = inv + _mm(inv, p)
    eg = jnp.exp(gc_c)
    rhs = jnp.concatenate([v * beta_c, kb * eg], axis=1)
    sol = _mm(inv, rhs)
    u, w = sol[:, :HEAD], sol[:, HEAD:]
    qg = q * eg
    kd = k * jnp.exp(gtot - gc_c)
    ws = _mm(jnp.concatenate([w, qg], axis=0), state)
    v_new = u - ws[:c]
    o = ws[c:] + _mm(qk, v_new)
    new_state = state * jnp.exp(gtot) + _mm_tn(kd, v_new)
    return o, new_state


def _gdn_scan_kernel(qkv_f_ref, qkv_b_ref, gcol_f_ref, gcol_b_ref, grow_f_ref, grow_b_ref,
                     of_ref, ob_ref, st_ref, *, n_heads, n_sub):
    @pl.when(pl.program_id(1) == 0)
    def _():
        st_ref[...] = jnp.zeros(st_ref.shape, F32)

    c = CHUNK
    ri = lax.broadcasted_iota(I32, (c, c), 0)
    ci = lax.broadcasted_iota(I32, (c, c), 1)
    eye = (ri == ci).astype(F32)
    tri = {False: (ri > ci, ri >= ci, eye), True: (ri < ci, ri <= ci, eye)}
    lower_incl = (ri >= ci).astype(F32)
    upper_incl = (ri <= ci).astype(F32)
    ng = N_DIR * n_heads

    for d in range(N_DIR):
        reverse = d == 1
        qkv_ref = qkv_b_ref if reverse else qkv_f_ref
        gcol_ref = gcol_b_ref if reverse else gcol_f_ref
        grow_ref = grow_b_ref if reverse else grow_f_ref
        o_ref = ob_ref if reverse else of_ref
        subs = range(n_sub - 1, -1, -1) if reverse else range(n_sub)
        for sub in subs:
            r0 = sub * c
            gcol = gcol_ref[0, r0:r0 + c, :]
            grow = grow_ref[0, ng:2 * ng, r0:r0 + c]
            gc_col = jnp.dot(upper_incl if reverse else lower_incl, gcol,
                             precision=HI, preferred_element_type=F32)
            gc_row = jnp.dot(grow, lower_incl if reverse else upper_incl,
                             precision=HI, preferred_element_type=F32)
            last = 0 if reverse else c - 1
            for h in range(n_heads):
                gi = d * n_heads + h
                q = qkv_ref[0, r0:r0 + c, h * HEAD:(h + 1) * HEAD]
                k = qkv_ref[0, r0:r0 + c, (n_heads + h) * HEAD:(n_heads + h + 1) * HEAD]
                v = qkv_ref[0, r0:r0 + c, (2 * n_heads + h) * HEAD:(2 * n_heads + h + 1) * HEAD]
                beta_c = gcol[:, gi:gi + 1]
                gc_c = gc_col[:, ng + gi:ng + gi + 1]
                gc_r = gc_row[gi:gi + 1, :]
                gtot = gc_r[:, last:last + 1]
                o, new_state = _gdn_chunk(q, k, v, beta_c, gc_c, gc_r, gtot, st_ref[gi],
                                          reverse=reverse, tri=tri)
                st_ref[gi] = new_state
                o_ref[0, r0:r0 + c, h * HEAD:(h + 1) * HEAD] = o


def _gdn_scan(qkv, gcol, grow, n_heads, *, rows):
    b, s, _ = qkv.shape
    nblk = s // rows
    n_sub = rows // CHUNK
    dg = n_heads * HEAD
    fwd = lambda i, t: (i, t, 0)
    bwd = lambda i, t: (i, nblk - 1 - t, 0)
    out = jax.ShapeDtypeStruct((b, s, dg), F32)
    return pl.pallas_call(
        functools.partial(_gdn_scan_kernel, n_heads=n_heads, n_sub=n_sub),
        out_shape=(out, out),
        grid=(b, nblk),
        in_specs=[pl.BlockSpec((1, rows, 3 * dg), fwd),
                  pl.BlockSpec((1, rows, 3 * dg), bwd),
                  pl.BlockSpec((1, rows, LANES), fwd),
                  pl.BlockSpec((1, rows, LANES), bwd),
                  pl.BlockSpec((1, LANES, rows), lambda i, t: (i, 0, t)),
                  pl.BlockSpec((1, LANES, rows), lambda i, t: (i, 0, nblk - 1 - t))],
        out_specs=(pl.BlockSpec((1, rows, dg), fwd), pl.BlockSpec((1, rows, dg), bwd)),
        scratch_shapes=[pltpu.VMEM((N_DIR * n_heads, HEAD, HEAD), F32)],
        compiler_params=_cparams(("parallel", "arbitrary")),
        name="gdn_scan",
    )(qkv, qkv, gcol, gcol, grow, grow)


def _diff_attn_kernel(lam_ref, q_ref, k_ref, v_ref, o_ref):
    lam = lam_ref[0]
    q = q_ref[0]
    k = k_ref[0]
    v = v_ref[0]
    probs = []
    for t in range(2):
        s = lax.dot_general(q[:, t * HEAD:(t + 1) * HEAD], k[:, t * HEAD:(t + 1) * HEAD],
                            (((1,), (1,)), ((), ())), preferred_element_type=F32)
        m = jnp.max(s, axis=-1, keepdims=True)
        e = jnp.exp(s - m)
        probs.append(e * (1.0 / jnp.sum(e, axis=-1, keepdims=True)))
    wts = (probs[0] - lam * probs[1]).astype(BF16)
    o_ref[0] = jnp.dot(wts, v, preferred_element_type=F32)


def _diff_attn(qkv3d, lam, n_heads, *, tq):
    b, s, _ = qkv3d.shape
    dv = 2 * HEAD
    return pl.pallas_call(
        _diff_attn_kernel,
        out_shape=jax.ShapeDtypeStruct((b, s, n_heads * dv), F32),
        grid_spec=pltpu.PrefetchScalarGridSpec(
            num_scalar_prefetch=0,
            grid=(b, n_heads, s // tq),
            in_specs=[pl.BlockSpec(memory_space=pltpu.SMEM),
                      pl.BlockSpec((1, tq, dv), lambda i, h, t: (i, t, h)),
                      pl.BlockSpec((1, s, dv), lambda i, h, t: (i, 0, n_heads + h)),
                      pl.BlockSpec((1, s, dv), lambda i, h, t: (i, 0, 2 * n_heads + h))],
            out_specs=pl.BlockSpec((1, tq, dv), lambda i, h, t: (i, t, h)),
        ),
        compiler_params=_cparams(("parallel", "parallel", "arbitrary")),
        name="diff_attn",
    )(lam, qkv3d, qkv3d, qkv3d)


def _mix_out_kernel(of_ref, ob_ref, z_ref, at_ref, x_ref, gw_ref, sw_ref, wo_ref, n2w_ref, wr_ref,
                    x1_ref, n2_ref, aff_ref, afft_ref, mix_scr, *, n_gdn_heads, n_diff_heads,
                    post_scale, n_experts):
    o = of_ref[...] + ob_ref[...]
    z = z_ref[...]
    gw = gw_ref[...]
    for h in range(n_gdn_heads):
        sl = slice(h * HEAD, (h + 1) * HEAD)
        seg = o[:, sl]
        y = seg * lax.rsqrt(jnp.mean(seg * seg, axis=-1, keepdims=True) + NORM_EPS) * gw
        zz = z[:, sl]
        mix_scr[:, sl] = (y * (zz * _sigmoid(zz))).astype(BF16)
    base = n_gdn_heads * HEAD
    at = at_ref[...]
    sw = sw_ref[...]
    dv = 2 * HEAD
    for h in range(n_diff_heads):
        seg = at[:, h * dv:(h + 1) * dv]
        y = seg * lax.rsqrt(jnp.mean(seg * seg, axis=-1, keepdims=True) + NORM_EPS) * sw
        mix_scr[:, base + h * dv:base + (h + 1) * dv] = (y * post_scale).astype(BF16)
    x1 = x_ref[...] + jnp.dot(mix_scr[...], wo_ref[...], preferred_element_type=F32)
    x1_ref[...] = x1
    n2 = x1 * lax.rsqrt(jnp.mean(x1 * x1, axis=-1, keepdims=True) + NORM_EPS) * n2w_ref[...]
    n2_ref[...] = n2
    logits = jnp.dot(n2, wr_ref[...], precision=HI, preferred_element_type=F32)
    lane = lax.broadcasted_iota(I32, logits.shape, 1)
    logits = jnp.where(lane < n_experts, logits, -jnp.inf)
    mx = jnp.max(logits, axis=-1, keepdims=True)
    e = jnp.exp(logits - mx)
    aff = e / jnp.sum(e, axis=-1, keepdims=True)
    aff_ref[...] = aff
    afft_ref[0] = aff.T


def _mix_out(o_f, o_b, proj_gdn, attn, x2d, gdn_norm_w, subln_w, w_out_bf16, norm2_w, w_router,
             *, seq, tm, post_scale):
    m, d = x2d.shape
    dg = o_f.shape[1]
    n_gdn_heads = dg // HEAD
    n_diff_heads = attn.shape[1] // (2 * HEAD)
    n_experts = w_router.shape[1]
    nsb = seq // tm
    gw = jnp.tile(gdn_norm_w.reshape(1, HEAD), (1, 1))
    sw = subln_w.reshape(1, 2 * HEAD)
    wr = jnp.zeros((d, LANES), F32).at[:, :n_experts].set(w_router)
    zblk = proj_gdn.shape[1] // dg - 1
    row = lambda i: (i, 0)
    const = lambda i: (0, 0)
    return pl.pallas_call(
        functools.partial(_mix_out_kernel, n_gdn_heads=n_gdn_heads, n_diff_heads=n_diff_heads,
                          post_scale=post_scale, n_experts=n_experts),
        out_shape=(jax.ShapeDtypeStruct((m, d), F32), jax.ShapeDtypeStruct((m, d), F32),
                   jax.ShapeDtypeStruct((m, LANES), F32),
                   jax.ShapeDtypeStruct((m // seq, LANES, seq), F32)),
        grid=(m // tm,),
        in_specs=[pl.BlockSpec((tm, dg), row), pl.BlockSpec((tm, dg), row),
                  pl.BlockSpec((tm, dg), lambda i: (i, zblk)),
                  pl.BlockSpec((tm, attn.shape[1]), row),
                  pl.BlockSpec((tm, d), row),
                  pl.BlockSpec((1, HEAD), const), pl.BlockSpec((1, 2 * HEAD), const),
                  pl.BlockSpec(w_out_bf16.shape, const),
                  pl.BlockSpec((1, d), const), pl.BlockSpec((d, LANES), const)],
        out_specs=(pl.BlockSpec((tm, d), row), pl.BlockSpec((tm, d), row),
                   pl.BlockSpec((tm, LANES), row),
                   pl.BlockSpec((1, LANES, tm), lambda i: (i // nsb, 0, i % nsb))),
        scratch_shapes=[pltpu.VMEM((tm, d), BF16)],
        compiler_params=_cparams(("parallel",)),
        name="mix_out",
    )(o_f, o_b, proj_gdn, attn, x2d, gw, sw, w_out_bf16, norm2_w.reshape(1, d), wr)


def _topk_kernel(afft_ref, idx_ref, gate_ref, pos_scr, *, cap):
    x = afft_ref[0]
    ne, s = x.shape
    bits = pltpu.bitcast(x, I32)

    def bisect(i, prefix):
        cand = prefix | (jnp.int32(1) << (30 - i))
        cnt = jnp.sum((bits >= cand).astype(I32), axis=-1, keepdims=True)
        return jnp.where(cnt >= cap, cand, prefix)

    thr = lax.fori_loop(0, 31, bisect, jnp.zeros((ne, 1), I32))
    gt = bits > thr
    eq = bits == thr
    need = cap - jnp.sum(gt.astype(I32), axis=-1, keepdims=True)

    ri = lax.broadcasted_iota(I32, (LANES, LANES), 0)
    ci = lax.broadcasted_iota(I32, (LANES, LANES), 1)
    upper_strict = (ri < ci).astype(BF16)

    def excl_prefix(mask):
        carry = jnp.zeros((ne, 1), F32)
        parts = []
        for blk in range(s // LANES):
            mb = mask[:, blk * LANES:(blk + 1) * LANES].astype(BF16)
            parts.append(jnp.dot(mb, upper_strict, preferred_element_type=F32) + carry)
            carry = carry + jnp.sum(mb.astype(F32), axis=-1, keepdims=True)
        return jnp.concatenate(parts, axis=1)

    eq_rank = excl_prefix(eq)
    sel = jnp.logical_or(gt, jnp.logical_and(eq, eq_rank < need.astype(F32)))
    pos = excl_prefix(sel)

    pos_scr[...] = jnp.where(sel, pos, -1.0)

    tb = min(s, 512)
    tok = lax.broadcasted_iota(I32, (1, tb), 1)
    slot = lax.broadcasted_iota(I32, (cap, tb), 0).astype(F32)
    zero = jnp.zeros((3, tb), F32)

    def per_expert(e, carry):
        r = jnp.zeros((8, cap), F32)
        for blk in range(s // tb):
            prow = pos_scr[pl.ds(e, 1), blk * tb:(blk + 1) * tb]
            onehot = (prow == slot).astype(BF16)
            a = afft_ref[0, pl.ds(e, 1), blk * tb:(blk + 1) * tb]
            a_hi = a.astype(BF16).astype(F32)
            a_mid = (a - a_hi).astype(BF16).astype(F32)
            a_lo = a - a_hi - a_mid
            t = tok + blk * tb
            lhs = jnp.concatenate([(t // 64).astype(F32), (t % 64).astype(F32),
                                   a_hi, a_mid, a_lo, zero], axis=0)
            r = r + _mm_nt(lhs, onehot)
        idx_ref[0, pl.ds(e, 1), :] = (r[0:1] * 64.0 + r[1:2]).astype(I32)
        gate_ref[0, pl.ds(e, 1), :] = r[2:3] + r[3:4] + r[4:5]
        return carry

    lax.fori_loop(0, ne, per_expert, 0)


def _topk(afft, n_experts, cap):
    b, _, s = afft.shape
    return pl.pallas_call(
        functools.partial(_topk_kernel, cap=cap),
        out_shape=(jax.ShapeDtypeStruct((b, n_experts, cap), I32),
                   jax.ShapeDtypeStruct((b, n_experts, cap), F32)),
        grid=(b,),
        in_specs=[pl.BlockSpec((1, n_experts, s), lambda i: (i, 0, 0))],
        out_specs=(pl.BlockSpec((1, n_experts, cap), lambda i: (i, 0, 0)),
                   pl.BlockSpec((1, n_experts, cap), lambda i: (i, 0, 0))),
        scratch_shapes=[pltpu.VMEM((n_experts, s), F32)],
        compiler_params=_cparams(("parallel",)),
        name="ec_topk",
    )(afft)


def _gather_kernel(idx_ref, src_ref, o_ref, buf, sem, *, cap, seq, n_experts):
    e = pl.program_id(0)
    b = pl.program_id(1)
    base = (b * n_experts + e) * cap

    def row_copy(c):
        tok = idx_ref[base + c]
        return pltpu.make_async_copy(src_ref.at[pl.ds(b * seq + tok, 1), :],
                                     buf.at[pl.ds(c, 1), :], sem)

    def start(c, carry):
        row_copy(c).start()
        return carry

    def wait(c, carry):
        row_copy(c).wait()
        return carry

    lax.fori_loop(0, cap, start, 0)
    lax.fori_loop(0, cap, wait, 0)
    o_ref[0] = buf[...].astype(o_ref.dtype)


def _gather_rows(idx_flat, src2d, *, n_batch, n_experts, cap, seq):
    d = src2d.shape[1]
    return pl.pallas_call(
        functools.partial(_gather_kernel, cap=cap, seq=seq, n_experts=n_experts),
        out_shape=jax.ShapeDtypeStruct((n_experts, n_batch * cap, d), BF16),
        grid_spec=pltpu.PrefetchScalarGridSpec(
            num_scalar_prefetch=1,
            grid=(n_experts, n_batch),
            in_specs=[pl.BlockSpec(memory_space=pl.ANY)],
            out_specs=pl.BlockSpec((1, cap, d), lambda e, b, idx: (e, b, 0)),
            scratch_shapes=[pltpu.VMEM((cap, d), F32), pltpu.SemaphoreType.DMA(())],
        ),
        compiler_params=_cparams(("arbitrary", "arbitrary")),
        name="ec_gather",
    )(idx_flat, src2d)


def _ffn_kernel(x_ref, wg_ref, wu_ref, wd_ref, gate_ref, o_ref):
    f = pl.program_id(2)
    x = x_ref[0]
    g = jnp.dot(x, wg_ref[0].astype(BF16), preferred_element_type=F32)
    u = jnp.dot(x, wu_ref[0].astype(BF16), preferred_element_type=F32)
    hid = (g * _sigmoid(g) * u).astype(BF16)
    part = jnp.dot(hid, wd_ref[0].astype(BF16), preferred_element_type=F32)

    @pl.when(f == 0)
    def _():
        o_ref[0] = part

    @pl.when(f > 0)
    def _():
        o_ref[0] += part

    @pl.when(f == pl.num_programs(2) - 1)
    def _():
        o_ref[0] = o_ref[0] * gate_ref[0]


def _expert_ffn(xs, w_gate, w_up, w_down, gate_col, *, tmr, tf):
    ne, rows, d = xs.shape
    dff = w_gate.shape[2]
    return pl.pallas_call(
        _ffn_kernel,
        out_shape=jax.ShapeDtypeStruct((ne, rows, d), F32),
        grid=(ne, rows // tmr, dff // tf),
        in_specs=[pl.BlockSpec((1, tmr, d), lambda e, r, f: (e, r, 0)),
                  pl.BlockSpec((1, d, tf), lambda e, r, f: (e, 0, f)),
                  pl.BlockSpec((1, d, tf), lambda e, r, f: (e, 0, f)),
                  pl.BlockSpec((1, tf, d), lambda e, r, f: (e, f, 0)),
                  pl.BlockSpec((1, tmr, 1), lambda e, r, f: (e, r, 0))],
        out_specs=pl.BlockSpec((1, tmr, d), lambda e, r, f: (e, r, 0)),
        compiler_params=_cparams(("parallel", "parallel", "arbitrary")),
        name="ec_ffn",
    )(xs, w_gate, w_up, w_down, gate_col)


def _rmsnorm_kernel(x_ref, w_ref, o_ref):
    x = x_ref[...]
    o_ref[...] = x * lax.rsqrt(jnp.mean(x * x, axis=-1, keepdims=True) + NORM_EPS) * w_ref[...]


def _rmsnorm(x2d, w, *, tm):
    m, d = x2d.shape
    return pl.pallas_call(
        _rmsnorm_kernel,
        out_shape=jax.ShapeDtypeStruct((m, d), F32),
        grid=(m // tm,),
        in_specs=[pl.BlockSpec((tm, d), lambda i: (i, 0)), pl.BlockSpec((1, d), lambda i: (0, 0))],
        out_specs=pl.BlockSpec((tm, d), lambda i: (i, 0)),
        compiler_params=_cparams(("parallel",)),
        name="final_norm",
    )(x2d, w.reshape(1, d))


def _pick(n, pref):
    t = min(n, pref)
    while n % t:
        t //= 2
    return t


def kernel(x, norm1_w, w_in, conv_w, a_log, dt_bias, gdn_norm_w, diff_lambda, diff_subln_w, w_out,
           norm2_w, w_router, w_gate, w_up, w_down, final_norm_w):
    bsz, seq, d = x.shape
    depth = w_in.shape[0]
    d_gdn = d // 2
    d_diff = d - d_gdn
    n_gdn_heads = d_gdn // HEAD
    n_diff_heads = d_diff // (2 * HEAD)
    n_gate = N_DIR * n_gdn_heads
    n_experts = w_router.shape[2]
    cap = EC_CAPACITY * seq // n_experts
    m = bsz * seq
    c_gate = 4 * d_gdn
    c_diff = c_gate + 2 * n_gate

    half = HEAD // 2
    inv_freq = ROPE_THETA ** (-jnp.arange(half, dtype=F32) / half)
    ang = jnp.arange(seq, dtype=F32)[:, None] * inv_freq[None, :]
    cos_t = jnp.concatenate([jnp.cos(ang), jnp.cos(ang)], axis=1)
    sin_t = jnp.concatenate([-jnp.sin(ang), jnp.sin(ang)], axis=1)

    tm_proj = _pick(seq, 1024)
    x2 = x.reshape(m, d)
    for l in range(depth):
        w_l = w_in[l]
        w_gdn = w_l[:, :c_gate].astype(BF16)
        w_gt = jnp.zeros((d, LANES), BF16).at[:, :2 * n_gate].set(w_l[:, c_gate:c_diff].astype(BF16))
        w_df = w_l[:, c_diff:].astype(BF16)

        proj_gdn = _norm_proj(x2, norm1_w[l], w_gdn, tm=tm_proj, tn=512, out_dtype=F32)
        gates = _norm_proj(x2, norm1_w[l], w_gt, tm=tm_proj, tn=LANES, out_dtype=F32)
        n_blk = d_diff // 512
        proj_diff = _norm_proj(x2, norm1_w[l], w_df, tm=tm_proj, tn=512, out_dtype=BF16,
                               rot=(cos_t, sin_t, seq, n_blk, 2 * n_blk))

        qkv = _gdn_prep(proj_gdn.reshape(bsz, seq, 4 * d_gdn), conv_w[l], n_gdn_heads)
        gcol, grow = _gate_prep(gates, a_log[l], dt_bias[l], seq=seq, tm=_pick(seq, 512))
        o_f, o_b = _gdn_scan(qkv, gcol.reshape(bsz, seq, LANES), grow, n_gdn_heads, rows=_pick(seq, SCAN_ROWS))

        lf = diff_lambda[l].astype(F32)
        lambda_init = 0.8 - 0.6 * math.exp(-0.3 * l)
        lam = (jnp.exp(jnp.sum(lf[0] * lf[1])) - jnp.exp(jnp.sum(lf[2] * lf[3])) + lambda_init)
        attn = _diff_attn(proj_diff.reshape(bsz, seq, 3 * d_diff), lam.reshape(1).astype(F32),
                          n_diff_heads, tq=_pick(seq, ATTN_TQ))

        x1, n2, _, afft = _mix_out(
            o_f.reshape(m, d_gdn), o_b.reshape(m, d_gdn), proj_gdn, attn.reshape(m, d_diff), x2,
            gdn_norm_w[l], diff_subln_w[l], w_out[l].astype(BF16), norm2_w[l], w_router[l],
            seq=seq, tm=_pick(seq, 256), post_scale=1.0 - lambda_init)

        idx, gate = _topk(afft, n_experts, cap)
        xs = _gather_rows(idx.reshape(-1), n2, n_batch=bsz, n_experts=n_experts, cap=cap, seq=seq)
        gate_col = gate.transpose(1, 0, 2).reshape(n_experts, bsz * cap, 1)
        y = _expert_ffn(xs, w_gate[l], w_up[l], w_down[l], gate_col,
                        tmr=_pick(bsz * cap, 1024), tf=256)
        y = y.reshape(n_experts, bsz, cap, d).transpose(1, 0, 2, 3).reshape(bsz, n_experts * cap, d)
        upd = jax.vmap(lambda ib, yb: jnp.zeros((seq, d), F32).at[ib].add(yb))(
            idx.reshape(bsz, n_experts * cap), y)
        x2 = x1 + upd.reshape(m, d)
    return _rmsnorm(x2, final_norm_w, tm=_pick(m, 512)).reshape(bsz, seq, d)
```

```python
import functools
import math

import jax
import jax.numpy as jnp
from jax import lax
from jax.experimental import pallas as pl
from jax.experimental.pallas import tpu as pltpu

F32 = jnp.float32
BF16 = jnp.bfloat16
I32 = jnp.int32

NORM_EPS = 1e-6
LANES = 128
SUBLANES = 8
HEAD = 128
GDN_CONV = 5
CHUNK = 64
N_DIR = 2
N_EXPERTS = 16
EC_CAPACITY = 2
ROPE_THETA = 10000.0
VMEM_LIMIT = 56 * 1024 * 1024
HI = lax.Precision.HIGHEST
SCAN_ROWS = 128
ATTN_TQ = 256
ATTN_KC = 512
FFN_TM = 1024
FFN_TF = 256
COMBINE_TB = 256
COMBINE_W = 64


def _cparams(sem):
    return pltpu.CompilerParams(dimension_semantics=sem, vmem_limit_bytes=VMEM_LIMIT)


def _sigmoid(x):
    return 1.0 / (1.0 + jnp.exp(-x))


def _mm(a, b):
    return jnp.dot(a.astype(BF16), b.astype(BF16), preferred_element_type=F32)


def _mm_nt(a, b):
    return lax.dot_general(a.astype(BF16), b.astype(BF16), (((1,), (1,)), ((), ())),
                           preferred_element_type=F32)


def _mm_tn(a, b):
    return lax.dot_general(a.astype(BF16), b.astype(BF16), (((0,), (0,)), ((), ())),
                           preferred_element_type=F32)


def _norm_rows(x_ref, nw_ref, n_scr):
    x = x_ref[...]
    ms = jnp.mean(x * x, axis=-1, keepdims=True)
    n_scr[...] = (x * lax.rsqrt(ms + NORM_EPS) * nw_ref[...]).astype(BF16)


def _norm_proj_kernel(x_ref, nw_ref, w_ref, o_ref, n_scr):
    @pl.when(pl.program_id(1) == 0)
    def _():
        _norm_rows(x_ref, nw_ref, n_scr)

    o_ref[...] = jnp.dot(n_scr[...], w_ref[...], preferred_element_type=F32).astype(o_ref.dtype)


def _norm_proj_rot_kernel(x_ref, nw_ref, w_ref, cos_ref, sin_ref, o_ref, n_scr, *,
                          n_q_blocks, n_rot_blocks):
    j = pl.program_id(1)

    @pl.when(j == 0)
    def _():
        _norm_rows(x_ref, nw_ref, n_scr)

    acc = jnp.dot(n_scr[...], w_ref[...], preferred_element_type=F32)
    tn = acc.shape[1]

    @pl.when(j < n_rot_blocks)
    def _():
        scale = jnp.where(j < n_q_blocks, HEAD ** -0.5, 1.0).astype(F32)
        c = cos_ref[...]
        s = sin_ref[...]
        for g in range(tn // HEAD):
            t = acc[:, g * HEAD:(g + 1) * HEAD]
            r = t * c + pltpu.roll(t, HEAD // 2, axis=1) * s
            o_ref[:, g * HEAD:(g + 1) * HEAD] = (r * scale).astype(o_ref.dtype)

    @pl.when(j >= n_rot_blocks)
    def _():
        o_ref[...] = acc.astype(o_ref.dtype)


def _norm_proj(x2d, nw, w_bf16, *, tm, tn, out_dtype, rot=None):
    m, d = x2d.shape
    n = w_bf16.shape[1]
    grid = (m // tm, n // tn)
    in_specs = [pl.BlockSpec((tm, d), lambda i, j: (i, 0)),
                pl.BlockSpec((1, d), lambda i, j: (0, 0)),
                pl.BlockSpec((d, tn), lambda i, j: (0, j))]
    args = [x2d, nw.reshape(1, d), w_bf16]
    if rot is None:
        kern = _norm_proj_kernel
    else:
        cos_t, sin_t, seq, n_q_blocks, n_rot_blocks = rot
        nsb = seq // tm
        in_specs += [pl.BlockSpec((tm, HEAD), lambda i, j: (i % nsb, 0)),
                     pl.BlockSpec((tm, HEAD), lambda i, j: (i % nsb, 0))]
        args += [cos_t, sin_t]
        kern = functools.partial(_norm_proj_rot_kernel, n_q_blocks=n_q_blocks,
                                 n_rot_blocks=n_rot_blocks)
    return pl.pallas_call(
        kern,
        out_shape=jax.ShapeDtypeStruct((m, n), out_dtype),
        grid=grid,
        in_specs=in_specs,
        out_specs=pl.BlockSpec((tm, tn), lambda i, j: (i, j)),
        scratch_shapes=[pltpu.VMEM((tm, d), BF16)],
        compiler_params=_cparams(("parallel", "arbitrary")),
        name="norm_proj" if rot is None else "norm_proj_rot",
    )(*args)


def _gdn_prep_kernel(x_ref, w_ref, o_ref, *, n_heads):
    j = pl.program_id(1)
    x = x_ref[0]
    w = w_ref[...]
    s = x.shape[0]
    row = lax.broadcasted_iota(I32, x.shape, 0)
    half = GDN_CONV // 2
    acc = x * w[half:half + 1, :]
    for k in range(GDN_CONV):
        off = k - half
        if off == 0:
            continue
        xs = pltpu.roll(x, (-off) % s, axis=0)
        valid = jnp.logical_and(row + off >= 0, row + off < s)
        acc = acc + jnp.where(valid, xs, 0.0) * w[k:k + 1, :]
    y = acc * _sigmoid(acc)
    ss = jnp.sum(y * y, axis=-1, keepdims=True)
    nrm = y * lax.rsqrt(ss + NORM_EPS)
    out = jnp.where(j < n_heads, nrm * (HEAD ** -0.5), jnp.where(j < 2 * n_heads, nrm, y))
    o_ref[0] = out


def _gdn_prep(proj3d, conv_w, n_heads):
    b, s, _ = proj3d.shape
    ncol = 3 * n_heads
    return pl.pallas_call(
        functools.partial(_gdn_prep_kernel, n_heads=n_heads),
        out_shape=jax.ShapeDtypeStruct((b, s, ncol * HEAD), F32),
        grid=(b, ncol),
        in_specs=[pl.BlockSpec((1, s, HEAD), lambda i, j: (i, 0, j)),
                  pl.BlockSpec((GDN_CONV, HEAD), lambda i, j: (0, j))],
        out_specs=pl.BlockSpec((1, s, HEAD), lambda i, j: (i, 0, j)),
        compiler_params=_cparams(("parallel", "parallel")),
        name="gdn_prep",
    )(proj3d, conv_w)


def _gate_prep_kernel(x_ref, alog_ref, dtb_ref, col_ref, row_ref, *, n_gate):
    x = x_ref[...]
    lane = lax.broadcasted_iota(I32, x.shape, 1)
    beta = _sigmoid(x)
    a = x + dtb_ref[...]
    sp = jnp.maximum(a, 0.0) + jnp.log(1.0 + jnp.exp(-jnp.abs(a)))
    g = -jnp.exp(alog_ref[...]) * sp
    y = jnp.where(lane < n_gate, beta, jnp.where(lane < 2 * n_gate, g, 0.0))
    col_ref[...] = y
    row_ref[0] = y.T


def _gate_prep(gates2d, a_log, dt_bias, *, seq, tm):
    m = gates2d.shape[0]
    n_gate = a_log.size
    nsb = seq // tm
    pad = lambda v: jnp.zeros((1, LANES), F32).at[0, n_gate:2 * n_gate].set(v.reshape(-1).astype(F32))
    return pl.pallas_call(
        functools.partial(_gate_prep_kernel, n_gate=n_gate),
        out_shape=(jax.ShapeDtypeStruct((m, LANES), F32),
                   jax.ShapeDtypeStruct((m // seq, LANES, seq), F32)),
        grid=(m // tm,),
        in_specs=[pl.BlockSpec((tm, LANES), lambda i: (i, 0)),
                  pl.BlockSpec((1, LANES), lambda i: (0, 0)),
                  pl.BlockSpec((1, LANES), lambda i: (0, 0))],
        out_specs=(pl.BlockSpec((tm, LANES), lambda i: (i, 0)),
                   pl.BlockSpec((1, LANES, tm), lambda i: (i // nsb, 0, i % nsb))),
        compiler_params=_cparams(("parallel",)),
        name="gate_prep",
    )(gates2d, pad(a_log), pad(dt_bias))


def _gdn_scan_kernel(qkv_f_ref, qkv_b_ref, gcol_f_ref, gcol_b_ref, grow_f_ref, grow_b_ref,
                     of_ref, ob_ref, st_ref, *, n_heads, n_sub):
    @pl.when(pl.program_id(1) == 0)
    def _():
        st_ref[...] = jnp.zeros(st_ref.shape, F32)

    c = CHUNK
    ri = lax.broadcasted_iota(I32, (c, c), 0)
    ci = lax.broadcasted_iota(I32, (c, c), 1)
    eye = (ri == ci).astype(F32)
    strict = {False: ri > ci, True: ri < ci}
    incl = {False: ri >= ci, True: ri <= ci}
    lower_incl = (ri >= ci).astype(F32)
    upper_incl = (ri <= ci).astype(F32)
    ng = N_DIR * n_heads

    items = []
    for d in range(N_DIR):
        reverse = d == 1
        qkv_ref = qkv_b_ref if reverse else qkv_f_ref
        gcol_ref = gcol_b_ref if reverse else gcol_f_ref
        grow_ref = grow_b_ref if reverse else grow_f_ref
        for sub in range(n_sub):
            r0 = sub * c
            gcol = gcol_ref[0, r0:r0 + c, :]
            grow = grow_ref[0, ng:2 * ng, r0:r0 + c]
            gc_col = jnp.dot(upper_incl if reverse else lower_incl, gcol,
                             precision=HI, preferred_element_type=F32)
            gc_row = jnp.dot(grow, lower_incl if reverse else upper_incl,
                             precision=HI, preferred_element_type=F32)
            last = 0 if reverse else c - 1
            for h in range(n_heads):
                gi = d * n_heads + h
                it = dict(d=d, sub=sub, h=h, gi=gi, r0=r0, reverse=reverse)
                it["q"] = qkv_ref[0, r0:r0 + c, h * HEAD:(h + 1) * HEAD]
                it["k"] = qkv_ref[0, r0:r0 + c, (n_heads + h) * HEAD:(n_heads + h + 1) * HEAD]
                it["v"] = qkv_ref[0, r0:r0 + c, (2 * n_heads + h) * HEAD:(2 * n_heads + h + 1) * HEAD]
                it["beta"] = gcol[:, gi:gi + 1]
                it["gc_c"] = gc_col[:, ng + gi:ng + gi + 1]
                it["gc_r"] = gc_row[gi:gi + 1, :]
                it["gtot"] = it["gc_r"][:, last:last + 1]
                items.append(it)

    for it in items:
        m_incl = incl[it["reverse"]]
        gdiff = it["gc_c"] - it["gc_r"]
        it["decay"] = jnp.where(m_incl, jnp.exp(jnp.where(m_incl, gdiff, 0.0)), 0.0)
        it["kb"] = it["k"] * it["beta"]
    for it in items:
        it["aq"] = _mm_nt(jnp.concatenate([it["kb"], it["q"]], axis=0), it["k"])
    for it in items:
        low = jnp.where(strict[it["reverse"]], it["aq"][:c] * it["decay"], 0.0)
        it["qk"] = it["aq"][c:] * it["decay"]
        it["p"] = low
        it["inv"] = eye - low
    for _ in range(int(math.log2(c)) - 1):
        for it in items:
            it["p"] = _mm(it["p"], it["p"])
        for it in items:
            it["inv"] = it["inv"] + _mm(it["inv"], it["p"])
    for it in items:
        eg = jnp.exp(it["gc_c"])
        rhs = jnp.concatenate([it["v"] * it["beta"], it["kb"] * eg], axis=1)
        it["qg"] = it["q"] * eg
        it["kd"] = it["k"] * jnp.exp(it["gtot"] - it["gc_c"])
        it["sol"] = _mm(it["inv"], rhs)

    for step in range(n_sub):
        cur = [it for it in items if it["sub"] == (n_sub - 1 - step if it["reverse"] else step)]
        for it in cur:
            it["state"] = st_ref[it["gi"]]
            w = it["sol"][:, HEAD:]
            it["ws"] = _mm(jnp.concatenate([w, it["qg"]], axis=0), it["state"])
        for it in cur:
            it["v_new"] = it["sol"][:, :HEAD] - it["ws"][:c]
        for it in cur:
            o = it["ws"][c:] + _mm(it["qk"], it["v_new"])
            o_ref = ob_ref if it["reverse"] else of_ref
            o_ref[0, it["r0"]:it["r0"] + c, it["h"] * HEAD:(it["h"] + 1) * HEAD] = o
        for it in cur:
            st_ref[it["gi"]] = it["state"] * jnp.exp(it["gtot"]) + _mm_tn(it["kd"], it["v_new"])


def _gdn_scan(qkv, gcol, grow, n_heads, *, rows):
    b, s, _ = qkv.shape
    nblk = s // rows
    n_sub = rows // CHUNK
    dg = n_heads * HEAD
    fwd = lambda i, t: (i, t, 0)
    bwd = lambda i, t: (i, nblk - 1 - t, 0)
    out = jax.ShapeDtypeStruct((b, s, dg), F32)
    return pl.pallas_call(
        functools.partial(_gdn_scan_kernel, n_heads=n_heads, n_sub=n_sub),
        out_shape=(out, out),
        grid=(b, nblk),
        in_specs=[pl.BlockSpec((1, rows, 3 * dg), fwd),
                  pl.BlockSpec((1, rows, 3 * dg), bwd),
                  pl.BlockSpec((1, rows, LANES), fwd),
                  pl.BlockSpec((1, rows, LANES), bwd),
                  pl.BlockSpec((1, LANES, rows), lambda i, t: (i, 0, t)),
                  pl.BlockSpec((1, LANES, rows), lambda i, t: (i, 0, nblk - 1 - t))],
        out_specs=(pl.BlockSpec((1, rows, dg), fwd), pl.BlockSpec((1, rows, dg), bwd)),
        scratch_shapes=[pltpu.VMEM((N_DIR * n_heads, HEAD, HEAD), F32)],
        compiler_params=_cparams(("parallel", "arbitrary")),
        name="gdn_scan",
    )(qkv, qkv, gcol, gcol, grow, grow)


def _diff_attn_kernel(lam_ref, q_ref, k_ref, v_ref, o_ref, *, kc):
    lam = lam_ref[0]
    q = q_ref[0]
    seq = k_ref.shape[1]
    nchunk = seq // kc
    halves = []
    for t in range(2):
        qt = q[:, t * HEAD:(t + 1) * HEAD]
        scores = []
        m_lane = None
        for c in range(nchunk):
            kt = k_ref[0, c * kc:(c + 1) * kc, t * HEAD:(t + 1) * HEAD]
            s = lax.dot_general(qt, kt, (((1,), (1,)), ((), ())), preferred_element_type=F32)
            scores.append(s)
            for g in range(kc // LANES):
                sg = s[:, g * LANES:(g + 1) * LANES]
                m_lane = sg if m_lane is None else jnp.maximum(m_lane, sg)
        m = jnp.max(m_lane, axis=-1, keepdims=True)
        l_lane = None
        acc = None
        for c in range(nchunk):
            e = jnp.exp(scores[c] - m)
            for g in range(kc // LANES):
                eg = e[:, g * LANES:(g + 1) * LANES]
                l_lane = eg if l_lane is None else l_lane + eg
            pv = jnp.dot(e.astype(BF16), v_ref[0, c * kc:(c + 1) * kc, :],
                         preferred_element_type=F32)
            acc = pv if acc is None else acc + pv
        halves.append(acc * (1.0 / jnp.sum(l_lane, axis=-1, keepdims=True)))
    o_ref[0] = halves[0] - lam * halves[1]


def _diff_attn(qkv3d, lam, n_heads, *, tq):
    b, s, _ = qkv3d.shape
    dv = 2 * HEAD
    return pl.pallas_call(
        functools.partial(_diff_attn_kernel, kc=_pick(s, ATTN_KC)),
        out_shape=jax.ShapeDtypeStruct((b, s, n_heads * dv), F32),
        grid_spec=pltpu.PrefetchScalarGridSpec(
            num_scalar_prefetch=0,
            grid=(b, n_heads, s // tq),
            in_specs=[pl.BlockSpec(memory_space=pltpu.SMEM),
                      pl.BlockSpec((1, tq, dv), lambda i, h, t: (i, t, h)),
                      pl.BlockSpec((1, s, dv), lambda i, h, t: (i, 0, n_heads + h)),
                      pl.BlockSpec((1, s, dv), lambda i, h, t: (i, 0, 2 * n_heads + h))],
            out_specs=pl.BlockSpec((1, tq, dv), lambda i, h, t: (i, t, h)),
        ),
        compiler_params=_cparams(("parallel", "parallel", "arbitrary")),
        name="diff_attn",
    )(lam, qkv3d, qkv3d, qkv3d)


def _mix_out_kernel(of_ref, ob_ref, z_ref, at_ref, x_ref, gw_ref, sw_ref, wo_ref, n2w_ref, wr_ref,
                    x1_ref, n2_ref, aff_ref, afft_ref, mix_scr, *, n_gdn_heads, n_diff_heads,
                    post_scale, n_experts):
    o = of_ref[...] + ob_ref[...]
    z = z_ref[...]
    gw = gw_ref[...]
    for h in range(n_gdn_heads):
        sl = slice(h * HEAD, (h + 1) * HEAD)
        seg = o[:, sl]
        y = seg * lax.rsqrt(jnp.mean(seg * seg, axis=-1, keepdims=True) + NORM_EPS) * gw
        zz = z[:, sl]
        mix_scr[:, sl] = (y * (zz * _sigmoid(zz))).astype(BF16)
    base = n_gdn_heads * HEAD
    at = at_ref[...]
    sw = sw_ref[...]
    dv = 2 * HEAD
    for h in range(n_diff_heads):
        seg = at[:, h * dv:(h + 1) * dv]
        y = seg * lax.rsqrt(jnp.mean(seg * seg, axis=-1, keepdims=True) + NORM_EPS) * sw
        mix_scr[:, base + h * dv:base + (h + 1) * dv] = (y * post_scale).astype(BF16)
    x1 = x_ref[...] + jnp.dot(mix_scr[...], wo_ref[...], preferred_element_type=F32)
    x1_ref[...] = x1
    n2 = x1 * lax.rsqrt(jnp.mean(x1 * x1, axis=-1, keepdims=True) + NORM_EPS) * n2w_ref[...]
    n2_ref[...] = n2
    wr = wr_ref[...]
    n_hi = n2.astype(BF16)
    n_lo = (n2 - n_hi.astype(F32)).astype(BF16)
    w_hi = wr.astype(BF16)
    w_lo = (wr - w_hi.astype(F32)).astype(BF16)
    logits = (jnp.dot(n_hi, w_hi, preferred_element_type=F32)
              + jnp.dot(n_hi, w_lo, preferred_element_type=F32)
              + jnp.dot(n_lo, w_hi, preferred_element_type=F32))
    lane = lax.broadcasted_iota(I32, logits.shape, 1)
    logits = jnp.where(lane < n_experts, logits, -jnp.inf)
    mx = jnp.max(logits, axis=-1, keepdims=True)
    e = jnp.exp(logits - mx)
    aff = e / jnp.sum(e, axis=-1, keepdims=True)
    aff_ref[...] = aff
    afft_ref[0] = aff.T


def _mix_out(o_f, o_b, proj_gdn, attn, x2d, gdn_norm_w, subln_w, w_out_bf16, norm2_w, w_router,
             *, seq, tm, post_scale):
    m, d = x2d.shape
    dg = o_f.shape[1]
    n_gdn_heads = dg // HEAD
    n_diff_heads = attn.shape[1] // (2 * HEAD)
    n_experts = w_router.shape[1]
    nsb = seq // tm
    gw = jnp.tile(gdn_norm_w.reshape(1, HEAD), (1, 1))
    sw = subln_w.reshape(1, 2 * HEAD)
    wr = jnp.zeros((d, LANES), F32).at[:, :n_experts].set(w_router)
    zblk = proj_gdn.shape[1] // dg - 1
    row = lambda i: (i, 0)
    const = lambda i: (0, 0)
    return pl.pallas_call(
        functools.partial(_mix_out_kernel, n_gdn_heads=n_gdn_heads, n_diff_heads=n_diff_heads,
                          post_scale=post_scale, n_experts=n_experts),
        out_shape=(jax.ShapeDtypeStruct((m, d), F32), jax.ShapeDtypeStruct((m, d), F32),
                   jax.ShapeDtypeStruct((m, LANES), F32),
                   jax.ShapeDtypeStruct((m // seq, LANES, seq), F32)),
        grid=(m // tm,),
        in_specs=[pl.BlockSpec((tm, dg), row), pl.BlockSpec((tm, dg), row),
                  pl.BlockSpec((tm, dg), lambda i: (i, zblk)),
                  pl.BlockSpec((tm, attn.shape[1]), row),
                  pl.BlockSpec((tm, d), row),
                  pl.BlockSpec((1, HEAD), const), pl.BlockSpec((1, 2 * HEAD), const),
                  pl.BlockSpec(w_out_bf16.shape, const),
                  pl.BlockSpec((1, d), const), pl.BlockSpec((d, LANES), const)],
        out_specs=(pl.BlockSpec((tm, d), row), pl.BlockSpec((tm, d), row),
                   pl.BlockSpec((tm, LANES), row),
                   pl.BlockSpec((1, LANES, tm), lambda i: (i // nsb, 0, i % nsb))),
        scratch_shapes=[pltpu.VMEM((tm, d), BF16)],
        compiler_params=_cparams(("parallel",)),
        name="mix_out",
    )(o_f, o_b, proj_gdn, attn, x2d, gw, sw, w_out_bf16, norm2_w.reshape(1, d), wr)


def _topk_kernel(afft_ref, idx_ref, gate_ref, post_ref, cs_ref, pos_scr, *, cap):
    x = afft_ref[0]
    ne, s = x.shape
    bits = pltpu.bitcast(x, I32)

    def bisect(i, prefix):
        cand = prefix | (jnp.int32(1) << (30 - i))
        cnt = jnp.sum((bits >= cand).astype(I32), axis=-1, keepdims=True)
        return jnp.where(cnt >= cap, cand, prefix)

    thr = lax.fori_loop(0, 31, bisect, jnp.zeros((ne, 1), I32))
    gt = bits > thr
    eq = bits == thr
    need = cap - jnp.sum(gt.astype(I32), axis=-1, keepdims=True)

    ri = lax.broadcasted_iota(I32, (LANES, LANES), 0)
    ci = lax.broadcasted_iota(I32, (LANES, LANES), 1)
    upper_strict = (ri < ci).astype(BF16)

    def excl_prefix(mask):
        carry = jnp.zeros((ne, 1), F32)
        parts, starts = [], []
        for blk in range(s // LANES):
            mb = mask[:, blk * LANES:(blk + 1) * LANES].astype(BF16)
            parts.append(jnp.dot(mb, upper_strict, preferred_element_type=F32) + carry)
            starts.append(carry)
            carry = carry + jnp.sum(mb.astype(F32), axis=-1, keepdims=True)
        return jnp.concatenate(parts, axis=1), jnp.concatenate(starts, axis=1)

    eq_rank, _ = excl_prefix(eq)
    sel = jnp.logical_or(gt, jnp.logical_and(eq, eq_rank < need.astype(F32)))
    pos, starts = excl_prefix(sel)

    posm = jnp.where(sel, pos, -1.0)
    pos_scr[...] = posm
    cs_ref[0] = starts.astype(I32)
    posm_pad = jnp.concatenate([posm, jnp.full((LANES - ne, s), -1.0, F32)], axis=0)
    for blk in range(s // LANES):
        post_ref[0, blk * LANES:(blk + 1) * LANES, :] = posm_pad[:, blk * LANES:(blk + 1) * LANES].T

    tb = min(s, 512)
    tok = lax.broadcasted_iota(I32, (1, tb), 1)
    slot = lax.broadcasted_iota(I32, (cap, tb), 0).astype(F32)
    zero = jnp.zeros((3, tb), F32)

    def per_expert(e, carry):
        r = jnp.zeros((8, cap), F32)
        for blk in range(s // tb):
            prow = pos_scr[pl.ds(e, 1), blk * tb:(blk + 1) * tb]
            onehot = (prow == slot).astype(BF16)
            a = afft_ref[0, pl.ds(e, 1), blk * tb:(blk + 1) * tb]
            a_hi = a.astype(BF16).astype(F32)
            a_mid = (a - a_hi).astype(BF16).astype(F32)
            a_lo = a - a_hi - a_mid
            t = tok + blk * tb
            lhs = jnp.concatenate([(t // 64).astype(F32), (t % 64).astype(F32),
                                   a_hi, a_mid, a_lo, zero], axis=0)
            r = r + _mm_nt(lhs, onehot)
        idx_ref[0, pl.ds(e, 1), :] = (r[0:1] * 64.0 + r[1:2]).astype(I32)
        gate_ref[0, pl.ds(e, 1), :] = r[2:3] + r[3:4] + r[4:5]
        return carry

    lax.fori_loop(0, ne, per_expert, 0)


def _topk(afft, n_experts, cap):
    b, _, s = afft.shape
    return pl.pallas_call(
        functools.partial(_topk_kernel, cap=cap),
        out_shape=(jax.ShapeDtypeStruct((b, n_experts, cap), I32),
                   jax.ShapeDtypeStruct((b, n_experts, cap), F32),
                   jax.ShapeDtypeStruct((b, s, LANES), F32),
                   jax.ShapeDtypeStruct((b, n_experts, s // LANES), I32)),
        grid=(b,),
        in_specs=[pl.BlockSpec((1, n_experts, s), lambda i: (i, 0, 0))],
        out_specs=(pl.BlockSpec((1, n_experts, cap), lambda i: (i, 0, 0)),
                   pl.BlockSpec((1, n_experts, cap), lambda i: (i, 0, 0)),
                   pl.BlockSpec((1, s, LANES), lambda i: (i, 0, 0)),
                   pl.BlockSpec((1, n_experts, s // LANES), lambda i: (i, 0, 0))),
        scratch_shapes=[pltpu.VMEM((n_experts, s), F32)],
        compiler_params=_cparams(("parallel",)),
        name="ec_topk",
    )(afft)


def _gather_kernel(idx_ref, src_ref, o_ref, buf, sem, *, cap, seq, n_experts, n_batch):
    step = pl.program_id(0) * n_batch + pl.program_id(1)
    nsteps = n_experts * n_batch
    cur = step % 2
    unroll = 8

    def row_copy(st, c, slot):
        e = st // n_batch
        b = st % n_batch
        tok = idx_ref[(b * n_experts + e) * cap + c]
        return pltpu.make_async_copy(src_ref.at[pl.ds(b * seq + tok, 1), :],
                                     buf.at[slot, pl.ds(c, 1), :], sem.at[slot])

    def issue(st, slot):
        def body(i, carry):
            for u in range(unroll):
                row_copy(st, i * unroll + u, slot).start()
            return carry
        lax.fori_loop(0, cap // unroll, body, 0)

    def drain(st, slot):
        def body(i, carry):
            for u in range(unroll):
                row_copy(st, i * unroll + u, slot).wait()
            return carry
        lax.fori_loop(0, cap // unroll, body, 0)

    @pl.when(step == 0)
    def _():
        issue(0, 0)

    @pl.when(step + 1 < nsteps)
    def _():
        issue(step + 1, 1 - cur)

    drain(step, cur)
    o_ref[0] = buf[cur].astype(o_ref.dtype)


def _gather_rows(idx_flat, src2d, *, n_batch, n_experts, cap, seq):
    d = src2d.shape[1]
    return pl.pallas_call(
        functools.partial(_gather_kernel, cap=cap, seq=seq, n_experts=n_experts, n_batch=n_batch),
        out_shape=jax.ShapeDtypeStruct((n_experts, n_batch * cap, d), BF16),
        grid_spec=pltpu.PrefetchScalarGridSpec(
            num_scalar_prefetch=1,
            grid=(n_experts, n_batch),
            in_specs=[pl.BlockSpec(memory_space=pl.ANY)],
            out_specs=pl.BlockSpec((1, cap, d), lambda e, b, idx: (e, b, 0)),
            scratch_shapes=[pltpu.VMEM((2, cap, d), F32), pltpu.SemaphoreType.DMA((2,))],
        ),
        compiler_params=_cparams(("arbitrary", "arbitrary")),
        name="ec_gather",
    )(idx_flat, src2d)


def _ffn_kernel(x_ref, wg_ref, wu_ref, wd_ref, gate_ref, o_ref):
    f = pl.program_id(2)
    x = x_ref[0]
    g = jnp.dot(x, wg_ref[0, 0].astype(BF16), preferred_element_type=F32)
    u = jnp.dot(x, wu_ref[0, 0].astype(BF16), preferred_element_type=F32)
    hid = (g * _sigmoid(g) * u).astype(BF16)
    part = jnp.dot(hid, wd_ref[0, 0].astype(BF16), preferred_element_type=F32)

    @pl.when(f == 0)
    def _():
        o_ref[0] = part

    @pl.when(f > 0)
    def _():
        o_ref[0] += part

    @pl.when(f == pl.num_programs(2) - 1)
    def _():
        o_ref[0] = o_ref[0] * gate_ref[0]


def _expert_ffn(xs, w_gate, w_up, w_down, gate_col, *, layer, tmr, tf):
    ne, rows, d = xs.shape
    dff = w_gate.shape[3]
    return pl.pallas_call(
        _ffn_kernel,
        out_shape=jax.ShapeDtypeStruct((ne, rows, d), F32),
        grid=(ne, rows // tmr, dff // tf),
        in_specs=[pl.BlockSpec((1, tmr, d), lambda e, r, f: (e, r, 0)),
                  pl.BlockSpec((1, 1, d, tf), lambda e, r, f: (layer, e, 0, f)),
                  pl.BlockSpec((1, 1, d, tf), lambda e, r, f: (layer, e, 0, f)),
                  pl.BlockSpec((1, 1, tf, d), lambda e, r, f: (layer, e, f, 0)),
                  pl.BlockSpec((1, tmr, 1), lambda e, r, f: (e, r, 0))],
        out_specs=pl.BlockSpec((1, tmr, d), lambda e, r, f: (e, r, 0)),
        compiler_params=_cparams(("parallel", "parallel", "arbitrary")),
        name="ec_ffn",
    )(xs, w_gate, w_up, w_down, gate_col)


def _combine_kernel(cs_ref, nr_ref, pos_ref, x1_ref, y_hbm, o_ref, slab, sem, *,
                    w, cap, ne, nblk):
    j = pl.program_id(1)
    step = pl.program_id(0) * nblk + j
    nsteps = pl.num_programs(0) * nblk
    cur = step % 2
    per = LANES // w

    def window(st, e, r):
        first = cs_ref[st * ne + e] + r * (w - SUBLANES)
        start = jnp.minimum((first // SUBLANES) * SUBLANES, cap - w)
        return first, pl.multiple_of(start, SUBLANES)

    def copy(st, e, r, buf):
        _, start = window(st, e, r)
        row = pl.multiple_of((st // nblk) * cap + start, SUBLANES)
        return pltpu.make_async_copy(y_hbm.at[e, pl.ds(row, w), :],
                                     slab.at[buf, pl.ds(e * w, w), :], sem.at[buf])

    def fetch(st, r, buf):
        for e in range(ne):
            copy(st, e, r, buf).start()

    def drain(st, r, buf):
        for e in range(ne):
            copy(st, e, r, buf).wait()

    @pl.when(step == 0)
    def _():
        fetch(0, 0, 0)

    drain(step, 0, cur)

    @pl.when(step + 1 < nsteps)
    def _():
        fetch(step + 1, 0, 1 - cur)

    pos = pos_ref[0]
    lane = lax.broadcasted_iota(I32, pos.shape, 1)
    lane_f = lane.astype(F32)

    def contrib(r, buf):
        cols = []
        for grp in range(ne // per):
            e = grp * per + per - 1
            first, start = window(step, e, r)
            pe = pos[:, e:e + 1]
            first_v = first.astype(F32)
            base_v = (start - (per - 1) * w).astype(F32)
            for q in range(per - 2, -1, -1):
                e = grp * per + q
                first, start = window(step, e, r)
                m = lane < (q + 1) * w
                pe = jnp.where(m, pos[:, e:e + 1], pe)
                first_v = jnp.where(m, first.astype(F32), first_v)
                base_v = jnp.where(m, (start - q * w).astype(F32), base_v)
            hit = jnp.logical_and(jnp.logical_and(pe >= first_v, pe < first_v + (w - SUBLANES)),
                                  pe - base_v == lane_f)
            cols.append(hit.astype(BF16))
        sel = jnp.concatenate(cols, axis=1)
        y = slab[buf]
        y_hi = y.astype(BF16)
        y_lo = (y - y_hi.astype(F32)).astype(BF16)
        return (jnp.dot(sel, y_hi, preferred_element_type=F32)
                + jnp.dot(sel, y_lo, preferred_element_type=F32))

    acc = x1_ref[...] + contrib(0, cur)

    def extra_round(r, acc):
        fetch(step, r, cur)
        drain(step, r, cur)
        return acc + contrib(r, cur)

    o_ref[...] = lax.fori_loop(1, nr_ref[step], extra_round, acc)


def _combine(cs_flat, nr_flat, post, x1, y, *, seq, tb, w, cap):
    m, d = x1.shape
    ne = y.shape[0]
    nblk = seq // tb
    return pl.pallas_call(
        functools.partial(_combine_kernel, w=w, cap=cap, ne=ne, nblk=nblk),
        out_shape=jax.ShapeDtypeStruct((m, d), F32),
        grid_spec=pltpu.PrefetchScalarGridSpec(
            num_scalar_prefetch=2,
            grid=(m // seq, nblk),
            in_specs=[pl.BlockSpec((1, tb, LANES), lambda b, j, cs, nr: (b, j, 0)),
                      pl.BlockSpec((tb, d), lambda b, j, cs, nr: (b * nblk + j, 0)),
                      pl.BlockSpec(memory_space=pl.ANY)],
            out_specs=pl.BlockSpec((tb, d), lambda b, j, cs, nr: (b * nblk + j, 0)),
            scratch_shapes=[pltpu.VMEM((2, ne * w, d), F32), pltpu.SemaphoreType.DMA((2,))],
        ),
        compiler_params=_cparams(("arbitrary", "arbitrary")),
        name="ec_combine",
    )(cs_flat, nr_flat, post, x1, y)


def _rmsnorm_kernel(x_ref, w_ref, o_ref):
    x = x_ref[...]
    o_ref[...] = x * lax.rsqrt(jnp.mean(x * x, axis=-1, keepdims=True) + NORM_EPS) * w_ref[...]


def _rmsnorm(x2d, w, *, tm):
    m, d = x2d.shape
    return pl.pallas_call(
        _rmsnorm_kernel,
        out_shape=jax.ShapeDtypeStruct((m, d), F32),
        grid=(m // tm,),
        in_specs=[pl.BlockSpec((tm, d), lambda i: (i, 0)), pl.BlockSpec((1, d), lambda i: (0, 0))],
        out_specs=pl.BlockSpec((tm, d), lambda i: (i, 0)),
        compiler_params=_cparams(("parallel",)),
        name="final_norm",
    )(x2d, w.reshape(1, d))


def _pick(n, pref):
    t = min(n, pref)
    while n % t:
        t //= 2
    return t


def kernel(x, norm1_w, w_in, conv_w, a_log, dt_bias, gdn_norm_w, diff_lambda, diff_subln_w, w_out,
           norm2_w, w_router, w_gate, w_up, w_down, final_norm_w):
    bsz, seq, d = x.shape
    depth = w_in.shape[0]
    d_gdn = d // 2
    d_diff = d - d_gdn
    n_gdn_heads = d_gdn // HEAD
    n_diff_heads = d_diff // (2 * HEAD)
    n_gate = N_DIR * n_gdn_heads
    n_experts = w_router.shape[2]
    cap = EC_CAPACITY * seq // n_experts
    m = bsz * seq
    c_gate = 4 * d_gdn
    c_diff = c_gate + 2 * n_gate

    half = HEAD // 2
    inv_freq = ROPE_THETA ** (-jnp.arange(half, dtype=F32) / half)
    ang = jnp.arange(seq, dtype=F32)[:, None] * inv_freq[None, :]
    cos_t = jnp.concatenate([jnp.cos(ang), jnp.cos(ang)], axis=1)
    sin_t = jnp.concatenate([-jnp.sin(ang), jnp.sin(ang)], axis=1)

    tm_proj = _pick(seq, 1024)
    x2 = x.reshape(m, d)
    for l in range(depth):
        w_l = w_in[l]
        w_gdn = w_l[:, :c_gate].astype(BF16)
        w_gt = jnp.zeros((d, LANES), BF16).at[:, :2 * n_gate].set(w_l[:, c_gate:c_diff].astype(BF16))
        w_df = w_l[:, c_diff:].astype(BF16)

        proj_gdn = _norm_proj(x2, norm1_w[l], w_gdn, tm=tm_proj, tn=512, out_dtype=F32)
        gates = _norm_proj(x2, norm1_w[l], w_gt, tm=tm_proj, tn=LANES, out_dtype=F32)
        n_blk = d_diff // 512
        proj_diff = _norm_proj(x2, norm1_w[l], w_df, tm=tm_proj, tn=512, out_dtype=BF16,
                               rot=(cos_t, sin_t, seq, n_blk, 2 * n_blk))

        qkv = _gdn_prep(proj_gdn.reshape(bsz, seq, 4 * d_gdn), conv_w[l], n_gdn_heads)
        gcol, grow = _gate_prep(gates, a_log[l], dt_bias[l], seq=seq, tm=_pick(seq, 512))
        o_f, o_b = _gdn_scan(qkv, gcol.reshape(bsz, seq, LANES), grow, n_gdn_heads, rows=_pick(seq, SCAN_ROWS))

        lf = diff_lambda[l].astype(F32)
        lambda_init = 0.8 - 0.6 * math.exp(-0.3 * l)
        lam = (jnp.exp(jnp.sum(lf[0] * lf[1])) - jnp.exp(jnp.sum(lf[2] * lf[3])) + lambda_init)
        attn = _diff_attn(proj_diff.reshape(bsz, seq, 3 * d_diff), lam.reshape(1).astype(F32),
                          n_diff_heads, tq=_pick(seq, ATTN_TQ))

        x1, n2, _, afft = _mix_out(
            o_f.reshape(m, d_gdn), o_b.reshape(m, d_gdn), proj_gdn, attn.reshape(m, d_diff), x2,
            gdn_norm_w[l], diff_subln_w[l], w_out[l].astype(BF16), norm2_w[l], w_router[l],
            seq=seq, tm=_pick(seq, 256), post_scale=1.0 - lambda_init)

        idx, gate, post, cs128 = _topk(afft, n_experts, cap)
        xs = _gather_rows(idx.reshape(-1), n2, n_batch=bsz, n_experts=n_experts, cap=cap, seq=seq)
        gate_col = gate.transpose(1, 0, 2).reshape(n_experts, bsz * cap, 1)
        y = _expert_ffn(xs, w_gate, w_up, w_down, gate_col, layer=l,
                        tmr=_pick(bsz * cap, FFN_TM), tf=FFN_TF)
        tb = _pick(seq, COMBINE_TB)
        win = min(COMBINE_W, cap)
        cs = cs128[:, :, ::tb // LANES].transpose(0, 2, 1)
        run = jnp.concatenate([cs[:, 1:], jnp.full((bsz, 1, n_experts), cap, I32)], axis=1) - cs
        per_round = win - SUBLANES
        rounds = jnp.maximum(1, jnp.max((run + per_round - 1) // per_round, axis=-1))
        x2 = _combine(cs.reshape(-1), rounds.reshape(-1).astype(I32), post, x1, y,
                      seq=seq, tb=tb, w=win, cap=cap)
    return _rmsnorm(x2, final_norm_w, tm=_pick(m, 512)).reshape(bsz, seq, d)
```

```python
import functools
import math

import jax
import jax.numpy as jnp
from jax import lax
from jax.experimental import pallas as pl
from jax.experimental.pallas import tpu as pltpu

F32 = jnp.float32
BF16 = jnp.bfloat16
I32 = jnp.int32

NORM_EPS = 1e-6
LANES = 128
SUBLANES = 8
HEAD = 128
GDN_CONV = 5
CHUNK = 64
N_DIR = 2
N_EXPERTS = 16
EC_CAPACITY = 2
ROPE_THETA = 10000.0
VMEM_LIMIT = 56 * 1024 * 1024
HI = lax.Precision.HIGHEST
SCAN_ROWS = 128
ATTN_TQ = 256
ATTN_KC = 512
PROJ_TN = 1024
FFN_TM = 1024
FFN_TF = 256
COMBINE_TB = 256
COMBINE_W = 64


def _cparams(sem):
    return pltpu.CompilerParams(dimension_semantics=sem, vmem_limit_bytes=VMEM_LIMIT)


def _sigmoid(x):
    return 1.0 / (1.0 + jnp.exp(-x))


def _mm(a, b):
    return jnp.dot(a.astype(BF16), b.astype(BF16), preferred_element_type=F32)


def _mm_nt(a, b):
    return lax.dot_general(a.astype(BF16), b.astype(BF16), (((1,), (1,)), ((), ())),
                           preferred_element_type=F32)


def _mm_tn(a, b):
    return lax.dot_general(a.astype(BF16), b.astype(BF16), (((0,), (0,)), ((), ())),
                           preferred_element_type=F32)


def _norm_rows(x_ref, nw_ref, n_scr):
    x = x_ref[...]
    ms = jnp.mean(x * x, axis=-1, keepdims=True)
    n_scr[...] = (x * lax.rsqrt(ms + NORM_EPS) * nw_ref[...]).astype(BF16)


def _norm_proj_kernel(x_ref, nw_ref, w_ref, o_ref, n_scr):
    @pl.when(pl.program_id(1) == 0)
    def _():
        _norm_rows(x_ref, nw_ref, n_scr)

    o_ref[...] = jnp.dot(n_scr[...], w_ref[...], preferred_element_type=F32).astype(o_ref.dtype)


def _norm_proj_rot_kernel(x_ref, nw_ref, w_ref, cos_ref, sin_ref, o_ref, n_scr, *,
                          n_q_blocks, n_rot_blocks):
    j = pl.program_id(1)

    @pl.when(j == 0)
    def _():
        _norm_rows(x_ref, nw_ref, n_scr)

    acc = jnp.dot(n_scr[...], w_ref[...], preferred_element_type=F32)
    tn = acc.shape[1]

    @pl.when(j < n_rot_blocks)
    def _():
        scale = jnp.where(j < n_q_blocks, HEAD ** -0.5, 1.0).astype(F32)
        c = cos_ref[...]
        s = sin_ref[...]
        for g in range(tn // HEAD):
            t = acc[:, g * HEAD:(g + 1) * HEAD]
            r = t * c + pltpu.roll(t, HEAD // 2, axis=1) * s
            o_ref[:, g * HEAD:(g + 1) * HEAD] = (r * scale).astype(o_ref.dtype)

    @pl.when(j >= n_rot_blocks)
    def _():
        o_ref[...] = acc.astype(o_ref.dtype)


def _norm_proj(x2d, nw, w_bf16, *, tm, tn, out_dtype, rot=None):
    m, d = x2d.shape
    n = w_bf16.shape[1]
    grid = (m // tm, n // tn)
    in_specs = [pl.BlockSpec((tm, d), lambda i, j: (i, 0)),
                pl.BlockSpec((1, d), lambda i, j: (0, 0)),
                pl.BlockSpec((d, tn), lambda i, j: (0, j))]
    args = [x2d, nw.reshape(1, d), w_bf16]
    if rot is None:
        kern = _norm_proj_kernel
    else:
        cos_t, sin_t, seq, n_q_blocks, n_rot_blocks = rot
        nsb = seq // tm
        in_specs += [pl.BlockSpec((tm, HEAD), lambda i, j: (i % nsb, 0)),
                     pl.BlockSpec((tm, HEAD), lambda i, j: (i % nsb, 0))]
        args += [cos_t, sin_t]
        kern = functools.partial(_norm_proj_rot_kernel, n_q_blocks=n_q_blocks,
                                 n_rot_blocks=n_rot_blocks)
    return pl.pallas_call(
        kern,
        out_shape=jax.ShapeDtypeStruct((m, n), out_dtype),
        grid=grid,
        in_specs=in_specs,
        out_specs=pl.BlockSpec((tm, tn), lambda i, j: (i, j)),
        scratch_shapes=[pltpu.VMEM((tm, d), BF16)],
        compiler_params=_cparams(("parallel", "arbitrary")),
        name="norm_proj" if rot is None else "norm_proj_rot",
    )(*args)


def _gdn_prep_kernel(x_ref, w_ref, o_ref, *, n_heads):
    j = pl.program_id(1)
    x = x_ref[0]
    w = w_ref[...]
    s = x.shape[0]
    row = lax.broadcasted_iota(I32, x.shape, 0)
    half = GDN_CONV // 2
    acc = x * w[half:half + 1, :]
    for k in range(GDN_CONV):
        off = k - half
        if off == 0:
            continue
        xs = pltpu.roll(x, (-off) % s, axis=0)
        valid = jnp.logical_and(row + off >= 0, row + off < s)
        acc = acc + jnp.where(valid, xs, 0.0) * w[k:k + 1, :]
    y = acc * _sigmoid(acc)
    ss = jnp.sum(y * y, axis=-1, keepdims=True)
    nrm = y * lax.rsqrt(ss + NORM_EPS)
    out = jnp.where(j < n_heads, nrm * (HEAD ** -0.5), jnp.where(j < 2 * n_heads, nrm, y))
    o_ref[0] = out


def _gdn_prep(proj3d, conv_w, n_heads):
    b, s, _ = proj3d.shape
    ncol = 3 * n_heads
    return pl.pallas_call(
        functools.partial(_gdn_prep_kernel, n_heads=n_heads),
        out_shape=jax.ShapeDtypeStruct((b, s, ncol * HEAD), F32),
        grid=(b, ncol),
        in_specs=[pl.BlockSpec((1, s, HEAD), lambda i, j: (i, 0, j)),
                  pl.BlockSpec((GDN_CONV, HEAD), lambda i, j: (0, j))],
        out_specs=pl.BlockSpec((1, s, HEAD), lambda i, j: (i, 0, j)),
        compiler_params=_cparams(("parallel", "parallel")),
        name="gdn_prep",
    )(proj3d, conv_w)


def _gate_prep_kernel(x_ref, alog_ref, dtb_ref, col_ref, row_ref, *, n_gate):
    x = x_ref[...]
    lane = lax.broadcasted_iota(I32, x.shape, 1)
    beta = _sigmoid(x)
    a = x + dtb_ref[...]
    sp = jnp.maximum(a, 0.0) + jnp.log(1.0 + jnp.exp(-jnp.abs(a)))
    g = -jnp.exp(alog_ref[...]) * sp
    y = jnp.where(lane < n_gate, beta, jnp.where(lane < 2 * n_gate, g, 0.0))
    col_ref[...] = y
    row_ref[0] = y.T


def _gate_prep(gates2d, a_log, dt_bias, *, seq, tm):
    m = gates2d.shape[0]
    n_gate = a_log.size
    nsb = seq // tm
    pad = lambda v: jnp.zeros((1, LANES), F32).at[0, n_gate:2 * n_gate].set(v.reshape(-1).astype(F32))
    return pl.pallas_call(
        functools.partial(_gate_prep_kernel, n_gate=n_gate),
        out_shape=(jax.ShapeDtypeStruct((m, LANES), F32),
                   jax.ShapeDtypeStruct((m // seq, LANES, seq), F32)),
        grid=(m // tm,),
        in_specs=[pl.BlockSpec((tm, LANES), lambda i: (i, 0)),
                  pl.BlockSpec((1, LANES), lambda i: (0, 0)),
                  pl.BlockSpec((1, LANES), lambda i: (0, 0))],
        out_specs=(pl.BlockSpec((tm, LANES), lambda i: (i, 0)),
                   pl.BlockSpec((1, LANES, tm), lambda i: (i // nsb, 0, i % nsb))),
        compiler_params=_cparams(("parallel",)),
        name="gate_prep",
    )(gates2d, pad(a_log), pad(dt_bias))


def _gdn_scan_kernel(qkv_f_ref, qkv_b_ref, gcol_f_ref, gcol_b_ref, grow_f_ref, grow_b_ref,
                     of_ref, ob_ref, st_ref, *, n_heads, n_sub):
    @pl.when(pl.program_id(1) == 0)
    def _():
        st_ref[...] = jnp.zeros(st_ref.shape, F32)

    c = CHUNK
    ri = lax.broadcasted_iota(I32, (c, c), 0)
    ci = lax.broadcasted_iota(I32, (c, c), 1)
    eye = (ri == ci).astype(F32)
    strict = {False: ri > ci, True: ri < ci}
    incl = {False: ri >= ci, True: ri <= ci}
    lower_incl = (ri >= ci).astype(F32)
    upper_incl = (ri <= ci).astype(F32)
    ng = N_DIR * n_heads

    items = []
    for d in range(N_DIR):
        reverse = d == 1
        qkv_ref = qkv_b_ref if reverse else qkv_f_ref
        gcol_ref = gcol_b_ref if reverse else gcol_f_ref
        grow_ref = grow_b_ref if reverse else grow_f_ref
        for sub in range(n_sub):
            r0 = sub * c
            gcol = gcol_ref[0, r0:r0 + c, :]
            grow = grow_ref[0, ng:2 * ng, r0:r0 + c]
            gc_col = jnp.dot(upper_incl if reverse else lower_incl, gcol,
                             precision=HI, preferred_element_type=F32)
            gc_row = jnp.dot(grow, lower_incl if reverse else upper_incl,
                             precision=HI, preferred_element_type=F32)
            last = 0 if reverse else c - 1
            for h in range(n_heads):
                gi = d * n_heads + h
                it = dict(d=d, sub=sub, h=h, gi=gi, r0=r0, reverse=reverse)
                it["q"] = qkv_ref[0, r0:r0 + c, h * HEAD:(h + 1) * HEAD]
                it["k"] = qkv_ref[0, r0:r0 + c, (n_heads + h) * HEAD:(n_heads + h + 1) * HEAD]
                it["v"] = qkv_ref[0, r0:r0 + c, (2 * n_heads + h) * HEAD:(2 * n_heads + h + 1) * HEAD]
                it["beta"] = gcol[:, gi:gi + 1]
                it["gc_c"] = gc_col[:, ng + gi:ng + gi + 1]
                it["gc_r"] = gc_row[gi:gi + 1, :]
                it["gtot"] = it["gc_r"][:, last:last + 1]
                items.append(it)

    for it in items:
        m_incl = incl[it["reverse"]]
        gdiff = it["gc_c"] - it["gc_r"]
        it["decay"] = jnp.where(m_incl, jnp.exp(jnp.where(m_incl, gdiff, 0.0)), 0.0)
        it["kb"] = it["k"] * it["beta"]
    for it in items:
        it["aq"] = _mm_nt(jnp.concatenate([it["kb"], it["q"]], axis=0), it["k"])
    for it in items:
        low = jnp.where(strict[it["reverse"]], it["aq"][:c] * it["decay"], 0.0)
        it["qk"] = it["aq"][c:] * it["decay"]
        it["p"] = low
        it["inv"] = eye - low
    for _ in range(int(math.log2(c)) - 1):
        for it in items:
            it["p"] = _mm(it["p"], it["p"])
        for it in items:
            it["inv"] = it["inv"] + _mm(it["inv"], it["p"])
    for it in items:
        eg = jnp.exp(it["gc_c"])
        rhs = jnp.concatenate([it["v"] * it["beta"], it["kb"] * eg], axis=1)
        it["qg"] = it["q"] * eg
        it["kd"] = it["k"] * jnp.exp(it["gtot"] - it["gc_c"])
        it["sol"] = _mm(it["inv"], rhs)

    for step in range(n_sub):
        cur = [it for it in items if it["sub"] == (n_sub - 1 - step if it["reverse"] else step)]
        for it in cur:
            it["state"] = st_ref[it["gi"]]
            w = it["sol"][:, HEAD:]
            it["ws"] = _mm(jnp.concatenate([w, it["qg"]], axis=0), it["state"])
        for it in cur:
            it["v_new"] = it["sol"][:, :HEAD] - it["ws"][:c]
        for it in cur:
            o = it["ws"][c:] + _mm(it["qk"], it["v_new"])
            o_ref = ob_ref if it["reverse"] else of_ref
            o_ref[0, it["r0"]:it["r0"] + c, it["h"] * HEAD:(it["h"] + 1) * HEAD] = o
        for it in cur:
            st_ref[it["gi"]] = it["state"] * jnp.exp(it["gtot"]) + _mm_tn(it["kd"], it["v_new"])


def _gdn_scan(qkv, gcol, grow, n_heads, *, rows):
    b, s, _ = qkv.shape
    nblk = s // rows
    n_sub = rows // CHUNK
    dg = n_heads * HEAD
    fwd = lambda i, t: (i, t, 0)
    bwd = lambda i, t: (i, nblk - 1 - t, 0)
    out = jax.ShapeDtypeStruct((b, s, dg), F32)
    return pl.pallas_call(
        functools.partial(_gdn_scan_kernel, n_heads=n_heads, n_sub=n_sub),
        out_shape=(out, out),
        grid=(b, nblk),
        in_specs=[pl.BlockSpec((1, rows, 3 * dg), fwd),
                  pl.BlockSpec((1, rows, 3 * dg), bwd),
                  pl.BlockSpec((1, rows, LANES), fwd),
                  pl.BlockSpec((1, rows, LANES), bwd),
                  pl.BlockSpec((1, LANES, rows), lambda i, t: (i, 0, t)),
                  pl.BlockSpec((1, LANES, rows), lambda i, t: (i, 0, nblk - 1 - t))],
        out_specs=(pl.BlockSpec((1, rows, dg), fwd), pl.BlockSpec((1, rows, dg), bwd)),
        scratch_shapes=[pltpu.VMEM((N_DIR * n_heads, HEAD, HEAD), F32)],
        compiler_params=_cparams(("parallel", "arbitrary")),
        name="gdn_scan",
    )(qkv, qkv, gcol, gcol, grow, grow)


def _diff_attn_kernel(lam_ref, q_ref, k_ref, v_ref, o_ref, s_scr, m_scr, *, kc):
    @pl.when(jnp.logical_and(jnp.logical_and(pl.program_id(0) == 0, pl.program_id(1) == 0),
                             pl.program_id(2) == 0))
    def _():
        s_scr[...] = jnp.zeros(s_scr.shape, F32)
        m_scr[...] = jnp.zeros(m_scr.shape, F32)

    for parity in range(2):
        @pl.when(pl.program_id(2) % 2 == parity)
        def _():
            _diff_attn_step(lam_ref, q_ref, k_ref, v_ref, o_ref, s_scr, m_scr,
                            cur=parity, prev=1 - parity, kc=kc)


def _diff_attn_step(lam_ref, q_ref, k_ref, v_ref, o_ref, s_scr, m_scr, *, cur, prev, kc):
    lam = lam_ref[0]
    q = q_ref[0]
    seq = k_ref.shape[1]
    nchunk = seq // kc
    halves = []
    for t in range(2):
        qt = q[:, t * HEAD:(t + 1) * HEAD]
        m_prev = m_scr[prev, t]
        m_lane = None
        l_lane = None
        acc = None
        for c in range(nchunk):
            kt = k_ref[0, c * kc:(c + 1) * kc, t * HEAD:(t + 1) * HEAD]
            s = lax.dot_general(qt, kt, (((1,), (1,)), ((), ())), preferred_element_type=F32)
            s_scr[cur, t, :, c * kc:(c + 1) * kc] = s
            for g in range(kc // LANES):
                sg = s[:, g * LANES:(g + 1) * LANES]
                m_lane = sg if m_lane is None else jnp.maximum(m_lane, sg)
            sp = s_scr[prev, t, :, c * kc:(c + 1) * kc]
            e = jnp.exp(sp - jnp.concatenate([m_prev] * (kc // LANES), axis=1))
            for g in range(kc // LANES):
                eg = e[:, g * LANES:(g + 1) * LANES]
                l_lane = eg if l_lane is None else l_lane + eg
            pv = jnp.dot(e.astype(BF16), v_ref[0, c * kc:(c + 1) * kc, :],
                         preferred_element_type=F32)
            acc = pv if acc is None else acc + pv
        m_scr[cur, t] = jnp.broadcast_to(jnp.max(m_lane, axis=-1, keepdims=True), m_lane.shape)
        halves.append(acc * (1.0 / jnp.sum(l_lane, axis=-1, keepdims=True)))
    o_ref[0] = halves[0] - lam * halves[1]


def _diff_attn(qkv3d, lam, n_heads, *, tq):
    b, s, _ = qkv3d.shape
    dv = 2 * HEAD
    nq = s // tq
    return pl.pallas_call(
        functools.partial(_diff_attn_kernel, kc=_pick(s, ATTN_KC)),
        out_shape=jax.ShapeDtypeStruct((b, s, n_heads * dv), F32),
        grid_spec=pltpu.PrefetchScalarGridSpec(
            num_scalar_prefetch=0,
            grid=(b, n_heads, nq + 1),
            in_specs=[pl.BlockSpec(memory_space=pltpu.SMEM),
                      pl.BlockSpec((1, tq, dv), lambda i, h, t: (i, jnp.minimum(t, nq - 1), h)),
                      pl.BlockSpec((1, s, dv), lambda i, h, t: (i, 0, n_heads + h)),
                      pl.BlockSpec((1, s, dv), lambda i, h, t: (i, 0, 2 * n_heads + h))],
            out_specs=pl.BlockSpec((1, tq, dv), lambda i, h, t: (i, jnp.maximum(t - 1, 0), h)),
            scratch_shapes=[pltpu.VMEM((2, 2, tq, s), F32), pltpu.VMEM((2, 2, tq, LANES), F32)],
        ),
        compiler_params=_cparams(("arbitrary", "arbitrary", "arbitrary")),
        name="diff_attn",
    )(lam, qkv3d, qkv3d, qkv3d)


def _mix_out_kernel(of_ref, ob_ref, z_ref, at_ref, x_ref, gw_ref, sw_ref, wo_ref, n2w_ref, wr_ref,
                    x1_ref, n2_ref, aff_ref, afft_ref, mix_scr, *, n_gdn_heads, n_diff_heads,
                    post_scale, n_experts):
    o = of_ref[...] + ob_ref[...]
    z = z_ref[...]
    gw = gw_ref[...]
    for h in range(n_gdn_heads):
        sl = slice(h * HEAD, (h + 1) * HEAD)
        seg = o[:, sl]
        y = seg * lax.rsqrt(jnp.mean(seg * seg, axis=-1, keepdims=True) + NORM_EPS) * gw
        zz = z[:, sl]
        mix_scr[:, sl] = (y * (zz * _sigmoid(zz))).astype(BF16)
    base = n_gdn_heads * HEAD
    at = at_ref[...]
    sw = sw_ref[...]
    dv = 2 * HEAD
    for h in range(n_diff_heads):
        seg = at[:, h * dv:(h + 1) * dv]
        y = seg * lax.rsqrt(jnp.mean(seg * seg, axis=-1, keepdims=True) + NORM_EPS) * sw
        mix_scr[:, base + h * dv:base + (h + 1) * dv] = (y * post_scale).astype(BF16)
    x1 = x_ref[...] + jnp.dot(mix_scr[...], wo_ref[...], preferred_element_type=F32)
    x1_ref[...] = x1
    n2 = x1 * lax.rsqrt(jnp.mean(x1 * x1, axis=-1, keepdims=True) + NORM_EPS) * n2w_ref[...]
    n2_ref[...] = n2
    wr = wr_ref[...]
    n_hi = n2.astype(BF16)
    n_lo = (n2 - n_hi.astype(F32)).astype(BF16)
    w_hi = wr.astype(BF16)
    w_lo = (wr - w_hi.astype(F32)).astype(BF16)
    logits = (jnp.dot(n_hi, w_hi, preferred_element_type=F32)
              + jnp.dot(n_hi, w_lo, preferred_element_type=F32)
              + jnp.dot(n_lo, w_hi, preferred_element_type=F32))
    lane = lax.broadcasted_iota(I32, logits.shape, 1)
    logits = jnp.where(lane < n_experts, logits, -jnp.inf)
    mx = jnp.max(logits, axis=-1, keepdims=True)
    e = jnp.exp(logits - mx)
    aff = e / jnp.sum(e, axis=-1, keepdims=True)
    aff_ref[...] = aff
    afft_ref[0] = aff.T


def _mix_out(o_f, o_b, proj_gdn, attn, x2d, gdn_norm_w, subln_w, w_out_bf16, norm2_w, w_router,
             *, seq, tm, post_scale):
    m, d = x2d.shape
    dg = o_f.shape[1]
    n_gdn_heads = dg // HEAD
    n_diff_heads = attn.shape[1] // (2 * HEAD)
    n_experts = w_router.shape[1]
    nsb = seq // tm
    gw = jnp.tile(gdn_norm_w.reshape(1, HEAD), (1, 1))
    sw = subln_w.reshape(1, 2 * HEAD)
    wr = jnp.zeros((d, LANES), F32).at[:, :n_experts].set(w_router)
    zblk = proj_gdn.shape[1] // dg - 1
    row = lambda i: (i, 0)
    const = lambda i: (0, 0)
    return pl.pallas_call(
        functools.partial(_mix_out_kernel, n_gdn_heads=n_gdn_heads, n_diff_heads=n_diff_heads,
                          post_scale=post_scale, n_experts=n_experts),
        out_shape=(jax.ShapeDtypeStruct((m, d), F32), jax.ShapeDtypeStruct((m, d), F32),
                   jax.ShapeDtypeStruct((m, LANES), F32),
                   jax.ShapeDtypeStruct((m // seq, LANES, seq), F32)),
        grid=(m // tm,),
        in_specs=[pl.BlockSpec((tm, dg), row), pl.BlockSpec((tm, dg), row),
                  pl.BlockSpec((tm, dg), lambda i: (i, zblk)),
                  pl.BlockSpec((tm, attn.shape[1]), row),
                  pl.BlockSpec((tm, d), row),
                  pl.BlockSpec((1, HEAD), const), pl.BlockSpec((1, 2 * HEAD), const),
                  pl.BlockSpec(w_out_bf16.shape, const),
                  pl.BlockSpec((1, d), const), pl.BlockSpec((d, LANES), const)],
        out_specs=(pl.BlockSpec((tm, d), row), pl.BlockSpec((tm, d), row),
                   pl.BlockSpec((tm, LANES), row),
                   pl.BlockSpec((1, LANES, tm), lambda i: (i // nsb, 0, i % nsb))),
        scratch_shapes=[pltpu.VMEM((tm, d), BF16)],
        compiler_params=_cparams(("parallel",)),
        name="mix_out",
    )(o_f, o_b, proj_gdn, attn, x2d, gw, sw, w_out_bf16, norm2_w.reshape(1, d), wr)


def _topk_kernel(afft_ref, idx_ref, gate_ref, post_ref, cs_ref, pos_scr, *, cap):
    x = afft_ref[0]
    ne, s = x.shape
    bits = pltpu.bitcast(x, I32)

    def bisect(i, prefix):
        cand = prefix | (jnp.int32(1) << (30 - i))
        cnt = jnp.sum((bits >= cand).astype(I32), axis=-1, keepdims=True)
        return jnp.where(cnt >= cap, cand, prefix)

    thr = lax.fori_loop(0, 31, bisect, jnp.zeros((ne, 1), I32))
    gt = bits > thr
    eq = bits == thr
    need = cap - jnp.sum(gt.astype(I32), axis=-1, keepdims=True)

    ri = lax.broadcasted_iota(I32, (LANES, LANES), 0)
    ci = lax.broadcasted_iota(I32, (LANES, LANES), 1)
    upper_strict = (ri < ci).astype(BF16)

    def excl_prefix(mask):
        carry = jnp.zeros((ne, 1), F32)
        parts, starts = [], []
        for blk in range(s // LANES):
            mb = mask[:, blk * LANES:(blk + 1) * LANES].astype(BF16)
            parts.append(jnp.dot(mb, upper_strict, preferred_element_type=F32) + carry)
            starts.append(carry)
            carry = carry + jnp.sum(mb.astype(F32), axis=-1, keepdims=True)
        return jnp.concatenate(parts, axis=1), jnp.concatenate(starts, axis=1)

    eq_rank, _ = excl_prefix(eq)
    sel = jnp.logical_or(gt, jnp.logical_and(eq, eq_rank < need.astype(F32)))
    pos, starts = excl_prefix(sel)

    posm = jnp.where(sel, pos, -1.0)
    pos_scr[...] = posm
    cs_ref[0] = starts.astype(I32)
    posm_pad = jnp.concatenate([posm, jnp.full((LANES - ne, s), -1.0, F32)], axis=0)
    for blk in range(s // LANES):
        post_ref[0, blk * LANES:(blk + 1) * LANES, :] = posm_pad[:, blk * LANES:(blk + 1) * LANES].T

    tb = min(s, 512)
    tok = lax.broadcasted_iota(I32, (1, tb), 1)
    slot = lax.broadcasted_iota(I32, (cap, tb), 0).astype(F32)
    zero = jnp.zeros((3, tb), F32)

    def per_expert(e, carry):
        r = jnp.zeros((8, cap), F32)
        for blk in range(s // tb):
            prow = pos_scr[pl.ds(e, 1), blk * tb:(blk + 1) * tb]
            onehot = (prow == slot).astype(BF16)
            a = afft_ref[0, pl.ds(e, 1), blk * tb:(blk + 1) * tb]
            a_hi = a.astype(BF16).astype(F32)
            a_mid = (a - a_hi).astype(BF16).astype(F32)
            a_lo = a - a_hi - a_mid
            t = tok + blk * tb
            lhs = jnp.concatenate([(t // 64).astype(F32), (t % 64).astype(F32),
                                   a_hi, a_mid, a_lo, zero], axis=0)
            r = r + _mm_nt(lhs, onehot)
        idx_ref[0, pl.ds(e, 1), :] = (r[0:1] * 64.0 + r[1:2]).astype(I32)
        gate_ref[0, pl.ds(e, 1), :] = r[2:3] + r[3:4] + r[4:5]
        return carry

    lax.fori_loop(0, ne, per_expert, 0)


def _topk(afft, n_experts, cap):
    b, _, s = afft.shape
    return pl.pallas_call(
        functools.partial(_topk_kernel, cap=cap),
        out_shape=(jax.ShapeDtypeStruct((b, n_experts, cap), I32),
                   jax.ShapeDtypeStruct((b, n_experts, cap), F32),
                   jax.ShapeDtypeStruct((b, s, LANES), F32),
                   jax.ShapeDtypeStruct((b, n_experts, s // LANES), I32)),
        grid=(b,),
        in_specs=[pl.BlockSpec((1, n_experts, s), lambda i: (i, 0, 0))],
        out_specs=(pl.BlockSpec((1, n_experts, cap), lambda i: (i, 0, 0)),
                   pl.BlockSpec((1, n_experts, cap), lambda i: (i, 0, 0)),
                   pl.BlockSpec((1, s, LANES), lambda i: (i, 0, 0)),
                   pl.BlockSpec((1, n_experts, s // LANES), lambda i: (i, 0, 0))),
        scratch_shapes=[pltpu.VMEM((n_experts, s), F32)],
        compiler_params=_cparams(("parallel",)),
        name="ec_topk",
    )(afft)


def _gather_kernel(idx_ref, src_ref, o_ref, buf, sem, *, cap, seq, n_experts, n_batch):
    step = pl.program_id(0) * n_batch + pl.program_id(1)
    nsteps = n_experts * n_batch
    cur = step % 2
    unroll = 8

    def row_copy(st, c, slot):
        e = st // n_batch
        b = st % n_batch
        tok = idx_ref[(b * n_experts + e) * cap + c]
        return pltpu.make_async_copy(src_ref.at[pl.ds(b * seq + tok, 1), :],
                                     buf.at[slot, pl.ds(c, 1), :], sem.at[slot])

    def issue(st, slot):
        def body(i, carry):
            for u in range(unroll):
                row_copy(st, i * unroll + u, slot).start()
            return carry
        lax.fori_loop(0, cap // unroll, body, 0)

    def drain(st, slot):
        def body(i, carry):
            for u in range(unroll):
                row_copy(st, i * unroll + u, slot).wait()
            return carry
        lax.fori_loop(0, cap // unroll, body, 0)

    @pl.when(step == 0)
    def _():
        issue(0, 0)

    @pl.when(step + 1 < nsteps)
    def _():
        issue(step + 1, 1 - cur)

    drain(step, cur)
    o_ref[0] = buf[cur].astype(o_ref.dtype)


def _gather_rows(idx_flat, src2d, *, n_batch, n_experts, cap, seq):
    d = src2d.shape[1]
    return pl.pallas_call(
        functools.partial(_gather_kernel, cap=cap, seq=seq, n_experts=n_experts, n_batch=n_batch),
        out_shape=jax.ShapeDtypeStruct((n_experts, n_batch * cap, d), BF16),
        grid_spec=pltpu.PrefetchScalarGridSpec(
            num_scalar_prefetch=1,
            grid=(n_experts, n_batch),
            in_specs=[pl.BlockSpec(memory_space=pl.ANY)],
            out_specs=pl.BlockSpec((1, cap, d), lambda e, b, idx: (e, b, 0)),
            scratch_shapes=[pltpu.VMEM((2, cap, d), F32), pltpu.SemaphoreType.DMA((2,))],
        ),
        compiler_params=_cparams(("arbitrary", "arbitrary")),
        name="ec_gather",
    )(idx_flat, src2d)


def _ffn_kernel(x_ref, wg_ref, wu_ref, wd_ref, gate_ref, o_ref, acc_scr):
    f = pl.program_id(2)

    @pl.when(jnp.logical_and(jnp.logical_and(pl.program_id(0) == 0, pl.program_id(1) == 0), f == 0))
    def _():
        acc_scr[...] = jnp.zeros(acc_scr.shape, F32)

    x = x_ref[0]
    g = jnp.dot(x, wg_ref[0, 0].astype(BF16), preferred_element_type=F32)
    u = jnp.dot(x, wu_ref[0, 0].astype(BF16), preferred_element_type=F32)
    hid = (g * _sigmoid(g) * u).astype(BF16)
    part = jnp.dot(hid, wd_ref[0, 0].astype(BF16), preferred_element_type=F32)
    acc = jnp.where(f == 0, part, acc_scr[...] + part)
    acc_scr[...] = acc
    o_ref[0] = acc * gate_ref[0]


def _expert_ffn(xs, w_gate, w_up, w_down, gate_col, *, layer, tmr, tf):
    ne, rows, d = xs.shape
    dff = w_gate.shape[3]
    return pl.pallas_call(
        _ffn_kernel,
        out_shape=jax.ShapeDtypeStruct((ne, rows, d), F32),
        grid=(ne, rows // tmr, dff // tf),
        in_specs=[pl.BlockSpec((1, tmr, d), lambda e, r, f: (e, r, 0)),
                  pl.BlockSpec((1, 1, d, tf), lambda e, r, f: (layer, e, 0, f)),
                  pl.BlockSpec((1, 1, d, tf), lambda e, r, f: (layer, e, 0, f)),
                  pl.BlockSpec((1, 1, tf, d), lambda e, r, f: (layer, e, f, 0)),
                  pl.BlockSpec((1, tmr, 1), lambda e, r, f: (e, r, 0))],
        out_specs=pl.BlockSpec((1, tmr, d), lambda e, r, f: (e, r, 0)),
        scratch_shapes=[pltpu.VMEM((tmr, d), F32)],
        compiler_params=_cparams(("arbitrary", "arbitrary", "arbitrary")),
        name="ec_ffn",
    )(xs, w_gate, w_up, w_down, gate_col)


def _combine_kernel(cs_ref, nr_ref, pos_ref, x1_ref, fw_ref, y_hbm, o_ref, slab, sem, *,
                    w, cap, ne, nblk, final_norm):
    j = pl.program_id(1)
    step = pl.program_id(0) * nblk + j
    nsteps = pl.num_programs(0) * nblk
    cur = step % 2
    per = LANES // w

    def window(st, e, r):
        first = cs_ref[st * ne + e] + r * (w - SUBLANES)
        start = jnp.minimum((first // SUBLANES) * SUBLANES, cap - w)
        return first, pl.multiple_of(start, SUBLANES)

    def copy(st, e, r, buf):
        _, start = window(st, e, r)
        row = pl.multiple_of((st // nblk) * cap + start, SUBLANES)
        return pltpu.make_async_copy(y_hbm.at[e, pl.ds(row, w), :],
                                     slab.at[buf, pl.ds(e * w, w), :], sem.at[buf])

    def fetch(st, r, buf):
        for e in range(ne):
            copy(st, e, r, buf).start()

    def drain(st, r, buf):
        for e in range(ne):
            copy(st, e, r, buf).wait()

    @pl.when(step == 0)
    def _():
        fetch(0, 0, 0)

    drain(step, 0, cur)

    @pl.when(step + 1 < nsteps)
    def _():
        fetch(step + 1, 0, 1 - cur)

    pos = pos_ref[0]
    lane = lax.broadcasted_iota(I32, pos.shape, 1)
    lane_f = lane.astype(F32)

    def contrib(r, buf):
        cols = []
        for grp in range(ne // per):
            e = grp * per + per - 1
            first, start = window(step, e, r)
            pe = pos[:, e:e + 1]
            first_v = first.astype(F32)
            base_v = (start - (per - 1) * w).astype(F32)
            for q in range(per - 2, -1, -1):
                e = grp * per + q
                first, start = window(step, e, r)
                m = lane < (q + 1) * w
                pe = jnp.where(m, pos[:, e:e + 1], pe)
                first_v = jnp.where(m, first.astype(F32), first_v)
                base_v = jnp.where(m, (start - q * w).astype(F32), base_v)
            hit = jnp.logical_and(jnp.logical_and(pe >= first_v, pe < first_v + (w - SUBLANES)),
                                  pe - base_v == lane_f)
            cols.append(hit.astype(BF16))
        sel = jnp.concatenate(cols, axis=1)
        y = slab[buf]
        y_hi = y.astype(BF16)
        y_lo = (y - y_hi.astype(F32)).astype(BF16)
        return (jnp.dot(sel, y_hi, preferred_element_type=F32)
                + jnp.dot(sel, y_lo, preferred_element_type=F32))

    acc = x1_ref[...] + contrib(0, cur)

    def extra_round(r, acc):
        fetch(step, r, cur)
        drain(step, r, cur)
        return acc + contrib(r, cur)

    acc = lax.fori_loop(1, nr_ref[step], extra_round, acc)
    if final_norm:
        acc = acc * lax.rsqrt(jnp.mean(acc * acc, axis=-1, keepdims=True) + NORM_EPS) * fw_ref[...]
    o_ref[...] = acc


def _combine(cs_flat, nr_flat, post, x1, y, final_w, *, seq, tb, w, cap, final_norm):
    m, d = x1.shape
    ne = y.shape[0]
    nblk = seq // tb
    return pl.pallas_call(
        functools.partial(_combine_kernel, w=w, cap=cap, ne=ne, nblk=nblk, final_norm=final_norm),
        out_shape=jax.ShapeDtypeStruct((m, d), F32),
        grid_spec=pltpu.PrefetchScalarGridSpec(
            num_scalar_prefetch=2,
            grid=(m // seq, nblk),
            in_specs=[pl.BlockSpec((1, tb, LANES), lambda b, j, cs, nr: (b, j, 0)),
                      pl.BlockSpec((tb, d), lambda b, j, cs, nr: (b * nblk + j, 0)),
                      pl.BlockSpec((1, d), lambda b, j, cs, nr: (0, 0)),
                      pl.BlockSpec(memory_space=pl.ANY)],
            out_specs=pl.BlockSpec((tb, d), lambda b, j, cs, nr: (b * nblk + j, 0)),
            scratch_shapes=[pltpu.VMEM((2, ne * w, d), F32), pltpu.SemaphoreType.DMA((2,))],
        ),
        compiler_params=_cparams(("arbitrary", "arbitrary")),
        name="ec_combine",
    )(cs_flat, nr_flat, post, x1, final_w.reshape(1, d), y)


def _pick(n, pref):
    t = min(n, pref)
    while n % t:
        t //= 2
    return t


def kernel(x, norm1_w, w_in, conv_w, a_log, dt_bias, gdn_norm_w, diff_lambda, diff_subln_w, w_out,
           norm2_w, w_router, w_gate, w_up, w_down, final_norm_w):
    bsz, seq, d = x.shape
    depth = w_in.shape[0]
    d_gdn = d // 2
    d_diff = d - d_gdn
    n_gdn_heads = d_gdn // HEAD
    n_diff_heads = d_diff // (2 * HEAD)
    n_gate = N_DIR * n_gdn_heads
    n_experts = w_router.shape[2]
    cap = EC_CAPACITY * seq // n_experts
    m = bsz * seq
    c_gate = 4 * d_gdn
    c_diff = c_gate + 2 * n_gate

    half = HEAD // 2
    inv_freq = ROPE_THETA ** (-jnp.arange(half, dtype=F32) / half)
    ang = jnp.arange(seq, dtype=F32)[:, None] * inv_freq[None, :]
    cos_t = jnp.concatenate([jnp.cos(ang), jnp.cos(ang)], axis=1)
    sin_t = jnp.concatenate([-jnp.sin(ang), jnp.sin(ang)], axis=1)

    tm_proj = _pick(seq, 1024)
    x2 = x.reshape(m, d)
    for l in range(depth):
        w_l = w_in[l]
        w_gdn = w_l[:, :c_gate].astype(BF16)
        w_gt = jnp.zeros((d, LANES), BF16).at[:, :2 * n_gate].set(w_l[:, c_gate:c_diff].astype(BF16))
        w_df = w_l[:, c_diff:].astype(BF16)

        proj_gdn = _norm_proj(x2, norm1_w[l], w_gdn, tm=tm_proj, tn=PROJ_TN, out_dtype=F32)
        gates = _norm_proj(x2, norm1_w[l], w_gt, tm=tm_proj, tn=LANES, out_dtype=F32)
        n_blk = d_diff // PROJ_TN
        proj_diff = _norm_proj(x2, norm1_w[l], w_df, tm=tm_proj, tn=PROJ_TN, out_dtype=BF16,
                               rot=(cos_t, sin_t, seq, n_blk, 2 * n_blk))

        qkv = _gdn_prep(proj_gdn.reshape(bsz, seq, 4 * d_gdn), conv_w[l], n_gdn_heads)
        gcol, grow = _gate_prep(gates, a_log[l], dt_bias[l], seq=seq, tm=_pick(seq, 512))
        o_f, o_b = _gdn_scan(qkv, gcol.reshape(bsz, seq, LANES), grow, n_gdn_heads, rows=_pick(seq, SCAN_ROWS))

        lf = diff_lambda[l].astype(F32)
        lambda_init = 0.8 - 0.6 * math.exp(-0.3 * l)
        lam = (jnp.exp(jnp.sum(lf[0] * lf[1])) - jnp.exp(jnp.sum(lf[2] * lf[3])) + lambda_init)
        attn = _diff_attn(proj_diff.reshape(bsz, seq, 3 * d_diff), lam.reshape(1).astype(F32),
                          n_diff_heads, tq=_pick(seq, ATTN_TQ))

        x1, n2, _, afft = _mix_out(
            o_f.reshape(m, d_gdn), o_b.reshape(m, d_gdn), proj_gdn, attn.reshape(m, d_diff), x2,
            gdn_norm_w[l], diff_subln_w[l], w_out[l].astype(BF16), norm2_w[l], w_router[l],
            seq=seq, tm=_pick(seq, 256), post_scale=1.0 - lambda_init)

        idx, gate, post, cs128 = _topk(afft, n_experts, cap)
        xs = _gather_rows(idx.reshape(-1), n2, n_batch=bsz, n_experts=n_experts, cap=cap, seq=seq)
        gate_col = gate.transpose(1, 0, 2).reshape(n_experts, bsz * cap, 1)
        y = _expert_ffn(xs, w_gate, w_up, w_down, gate_col, layer=l,
                        tmr=_pick(bsz * cap, FFN_TM), tf=FFN_TF)
        tb = _pick(seq, COMBINE_TB)
        win = min(COMBINE_W, cap)
        cs = cs128[:, :, ::tb // LANES].transpose(0, 2, 1)
        run = jnp.concatenate([cs[:, 1:], jnp.full((bsz, 1, n_experts), cap, I32)], axis=1) - cs
        per_round = win - SUBLANES
        rounds = jnp.maximum(1, jnp.max((run + per_round - 1) // per_round, axis=-1))
        x2 = _combine(cs.reshape(-1), rounds.reshape(-1).astype(I32), post, x1, y, final_norm_w,
                      seq=seq, tb=tb, w=win, cap=cap, final_norm=l == depth - 1)
    return x2.reshape(bsz, seq, d)
```

```python
import functools
import math

import jax
import jax.numpy as jnp
from jax import lax
from jax.experimental import pallas as pl
from jax.experimental.pallas import tpu as pltpu

F32 = jnp.float32
BF16 = jnp.bfloat16
I32 = jnp.int32
U32 = jnp.uint32

NORM_EPS = 1e-6
LANES = 128
SUBLANES = 8
HEAD = 128
GDN_CONV = 5
CHUNK = 64
N_DIR = 2
N_EXPERTS = 16
EC_CAPACITY = 2
ROPE_THETA = 10000.0
VMEM_LIMIT = 56 * 1024 * 1024
HI = lax.Precision.HIGHEST
SCAN_ROWS = 256
PREP_ROWS = 512
ATTN_TQ = 256
ATTN_KC = 512
PROJ_TN = 1024
FFN_TM = 1024
FFN_TF = 256
COMBINE_TB = 256
COMBINE_W = 64


def _cparams(sem):
    return pltpu.CompilerParams(dimension_semantics=sem, vmem_limit_bytes=VMEM_LIMIT)


def _sigmoid(x):
    return 1.0 / (1.0 + jnp.exp(-x))


def _mm(a, b):
    return jnp.dot(a.astype(BF16), b.astype(BF16), preferred_element_type=F32)


def _mm_nt(a, b):
    return lax.dot_general(a.astype(BF16), b.astype(BF16), (((1,), (1,)), ((), ())),
                           preferred_element_type=F32)


def _mm_tn(a, b):
    return lax.dot_general(a.astype(BF16), b.astype(BF16), (((0,), (0,)), ((), ())),
                           preferred_element_type=F32)


def _norm_rows(x_ref, nw_ref, n_scr):
    x = x_ref[...]
    ms = jnp.mean(x * x, axis=-1, keepdims=True)
    n_scr[...] = (x * lax.rsqrt(ms + NORM_EPS) * nw_ref[...]).astype(BF16)


def _norm_proj_kernel(x_ref, nw_ref, w_ref, o_ref, n_scr):
    @pl.when(pl.program_id(1) == 0)
    def _():
        _norm_rows(x_ref, nw_ref, n_scr)

    o_ref[...] = jnp.dot(n_scr[...], w_ref[...], preferred_element_type=F32).astype(o_ref.dtype)


def _norm_proj_rot_kernel(x_ref, nw_ref, w_ref, cos_ref, sin_ref, o_ref, n_scr, *,
                          n_q_blocks, n_rot_blocks):
    j = pl.program_id(1)

    @pl.when(j == 0)
    def _():
        _norm_rows(x_ref, nw_ref, n_scr)

    acc = jnp.dot(n_scr[...], w_ref[...], preferred_element_type=F32)
    tn = acc.shape[1]
    is_rot = j < n_rot_blocks
    scale = jnp.where(j < n_q_blocks, HEAD ** -0.5, 1.0).astype(F32)
    c = cos_ref[...]
    s = sin_ref[...]
    for g in range(tn // HEAD):
        t = acc[:, g * HEAD:(g + 1) * HEAD]
        r = t * c + pltpu.roll(t, HEAD // 2, axis=1) * s
        o_ref[:, g * HEAD:(g + 1) * HEAD] = (jnp.where(is_rot, r, t) * scale).astype(o_ref.dtype)


def _norm_proj(x2d, nw, w_bf16, *, tm, tn, out_dtype, rot=None):
    m, d = x2d.shape
    n = w_bf16.shape[1]
    grid = (m // tm, n // tn)
    in_specs = [pl.BlockSpec((tm, d), lambda i, j: (i, 0)),
                pl.BlockSpec((1, d), lambda i, j: (0, 0)),
                pl.BlockSpec((d, tn), lambda i, j: (0, j))]
    args = [x2d, nw.reshape(1, d), w_bf16]
    if rot is None:
        kern = _norm_proj_kernel
    else:
        cos_t, sin_t, seq, n_q_blocks, n_rot_blocks = rot
        nsb = seq // tm
        in_specs += [pl.BlockSpec((tm, HEAD), lambda i, j: (i % nsb, 0)),
                     pl.BlockSpec((tm, HEAD), lambda i, j: (i % nsb, 0))]
        args += [cos_t, sin_t]
        kern = functools.partial(_norm_proj_rot_kernel, n_q_blocks=n_q_blocks,
                                 n_rot_blocks=n_rot_blocks)
    return pl.pallas_call(
        kern,
        out_shape=jax.ShapeDtypeStruct((m, n), out_dtype),
        grid=grid,
        in_specs=in_specs,
        out_specs=pl.BlockSpec((tm, tn), lambda i, j: (i, j)),
        scratch_shapes=[pltpu.VMEM((tm, d), BF16)],
        compiler_params=_cparams(("parallel", "arbitrary")),
        name="norm_proj" if rot is None else "norm_proj_rot",
    )(*args)


def _gdn_prep_kernel(x_ref, w_ref, o_ref, *, n_heads):
    j = pl.program_id(1)
    w = w_ref[...]
    s = x_ref.shape[1]
    rows = min(s, PREP_ROWS)
    half = GDN_CONV // 2
    row = lax.broadcasted_iota(I32, (rows, HEAD), 0)
    is_q = j < n_heads
    is_qk = j < 2 * n_heads
    for r0 in range(0, s, rows):
        acc = x_ref[0, r0:r0 + rows, :] * w[half:half + 1, :]
        for k in range(GDN_CONV):
            off = k - half
            if off == 0:
                continue
            if 0 <= r0 + off and r0 + off + rows <= s:
                xs = x_ref[0, r0 + off:r0 + off + rows, :]
            else:
                xs = pltpu.roll(x_ref[0, r0:r0 + rows, :], (-off) % rows, axis=0)
                valid = jnp.logical_and(row + (r0 + off) >= 0, row + (r0 + off) < s)
                xs = jnp.where(valid, xs, 0.0)
            acc = acc + xs * w[k:k + 1, :]
        y = acc * _sigmoid(acc)
        ss = jnp.sum(y * y, axis=-1, keepdims=True)
        nrm = y * lax.rsqrt(ss + NORM_EPS)
        out = jnp.where(is_q, nrm * (HEAD ** -0.5), jnp.where(is_qk, nrm, y))
        o_ref[0, r0:r0 + rows, :] = out


def _gdn_prep(proj3d, conv_w, n_heads):
    b, s, _ = proj3d.shape
    ncol = 3 * n_heads
    return pl.pallas_call(
        functools.partial(_gdn_prep_kernel, n_heads=n_heads),
        out_shape=jax.ShapeDtypeStruct((b, s, ncol * HEAD), F32),
        grid=(b, ncol),
        in_specs=[pl.BlockSpec((1, s, HEAD), lambda i, j: (i, 0, j)),
                  pl.BlockSpec((GDN_CONV, HEAD), lambda i, j: (0, j))],
        out_specs=pl.BlockSpec((1, s, HEAD), lambda i, j: (i, 0, j)),
        compiler_params=_cparams(("parallel", "parallel")),
        name="gdn_prep",
    )(proj3d, conv_w)


def _gate_prep_kernel(x_ref, alog_ref, dtb_ref, col_ref, row_ref, *, n_gate):
    x = x_ref[...]
    lane = lax.broadcasted_iota(I32, x.shape, 1)
    beta = _sigmoid(x)
    a = x + dtb_ref[...]
    sp = jnp.maximum(a, 0.0) + jnp.log(1.0 + jnp.exp(-jnp.abs(a)))
    g = -jnp.exp(alog_ref[...]) * sp
    y = jnp.where(lane < n_gate, beta, jnp.where(lane < 2 * n_gate, g, 0.0))
    col_ref[...] = y
    row_ref[0] = y.T


def _gate_prep(gates2d, a_log, dt_bias, *, seq, tm):
    m = gates2d.shape[0]
    n_gate = a_log.size
    nsb = seq // tm
    pad = lambda v: jnp.zeros((1, LANES), F32).at[0, n_gate:2 * n_gate].set(v.reshape(-1).astype(F32))
    return pl.pallas_call(
        functools.partial(_gate_prep_kernel, n_gate=n_gate),
        out_shape=(jax.ShapeDtypeStruct((m, LANES), F32),
                   jax.ShapeDtypeStruct((m // seq, LANES, seq), F32)),
        grid=(m // tm,),
        in_specs=[pl.BlockSpec((tm, LANES), lambda i: (i, 0)),
                  pl.BlockSpec((1, LANES), lambda i: (0, 0)),
                  pl.BlockSpec((1, LANES), lambda i: (0, 0))],
        out_specs=(pl.BlockSpec((tm, LANES), lambda i: (i, 0)),
                   pl.BlockSpec((1, LANES, tm), lambda i: (i // nsb, 0, i % nsb))),
        compiler_params=_cparams(("parallel",)),
        name="gate_prep",
    )(gates2d, pad(a_log), pad(dt_bias))


def _gdn_scan_kernel(qkv_f_ref, qkv_b_ref, gcol_f_ref, gcol_b_ref, grow_f_ref, grow_b_ref,
                     of_ref, ob_ref, st_ref, *, n_heads, n_sub):
    @pl.when(pl.program_id(1) == 0)
    def _():
        st_ref[...] = jnp.zeros(st_ref.shape, F32)

    c = CHUNK
    ri = lax.broadcasted_iota(I32, (c, c), 0)
    ci = lax.broadcasted_iota(I32, (c, c), 1)
    eye = (ri == ci).astype(F32)
    strict = {False: ri > ci, True: ri < ci}
    incl = {False: ri >= ci, True: ri <= ci}
    lower_incl = (ri >= ci).astype(F32)
    upper_incl = (ri <= ci).astype(F32)
    ng = N_DIR * n_heads

    items = []
    for d in range(N_DIR):
        reverse = d == 1
        qkv_ref = qkv_b_ref if reverse else qkv_f_ref
        gcol_ref = gcol_b_ref if reverse else gcol_f_ref
        grow_ref = grow_b_ref if reverse else grow_f_ref
        for sub in range(n_sub):
            r0 = sub * c
            gcol = gcol_ref[0, r0:r0 + c, :]
            grow = grow_ref[0, ng:2 * ng, r0:r0 + c]
            gc_col = jnp.dot(upper_incl if reverse else lower_incl, gcol,
                             precision=HI, preferred_element_type=F32)
            gc_row = jnp.dot(grow, lower_incl if reverse else upper_incl,
                             precision=HI, preferred_element_type=F32)
            last = 0 if reverse else c - 1
            for h in range(n_heads):
                gi = d * n_heads + h
                it = dict(d=d, sub=sub, h=h, gi=gi, r0=r0, reverse=reverse)
                it["q"] = qkv_ref[0, r0:r0 + c, h * HEAD:(h + 1) * HEAD]
                it["k"] = qkv_ref[0, r0:r0 + c, (n_heads + h) * HEAD:(n_heads + h + 1) * HEAD]
                it["v"] = qkv_ref[0, r0:r0 + c, (2 * n_heads + h) * HEAD:(2 * n_heads + h + 1) * HEAD]
                it["beta"] = gcol[:, gi:gi + 1]
                it["gc_c"] = gc_col[:, ng + gi:ng + gi + 1]
                it["gc_r"] = gc_row[gi:gi + 1, :]
                it["gtot"] = it["gc_r"][:, last:last + 1]
                items.append(it)

    for it in items:
        m_incl = incl[it["reverse"]]
        gdiff = it["gc_c"] - it["gc_r"]
        it["decay"] = jnp.where(m_incl, jnp.exp(jnp.where(m_incl, gdiff, 0.0)), 0.0)
        it["kb"] = it["k"] * it["beta"]
    for it in items:
        it["aq"] = _mm_nt(jnp.concatenate([it["kb"], it["q"]], axis=0), it["k"])
    for it in items:
        low = jnp.where(strict[it["reverse"]], it["aq"][:c] * it["decay"], 0.0)
        it["qk"] = it["aq"][c:] * it["decay"]
        it["p"] = low
        it["inv"] = eye - low
    for _ in range(int(math.log2(c)) - 1):
        for it in items:
            it["p"] = _mm(it["p"], it["p"])
        for it in items:
            it["inv"] = it["inv"] + _mm(it["inv"], it["p"])
    for it in items:
        eg = jnp.exp(it["gc_c"])
        rhs = jnp.concatenate([it["v"] * it["beta"], it["kb"] * eg], axis=1)
        it["qg"] = it["q"] * eg
        it["kd"] = it["k"] * jnp.exp(it["gtot"] - it["gc_c"])
        it["sol"] = _mm(it["inv"], rhs)

    for step in range(n_sub):
        cur = [it for it in items if it["sub"] == (n_sub - 1 - step if it["reverse"] else step)]
        for it in cur:
            it["state"] = st_ref[it["gi"]]
            w = it["sol"][:, HEAD:]
            it["ws"] = _mm(jnp.concatenate([w, it["qg"]], axis=0), it["state"])
        for it in cur:
            it["v_new"] = it["sol"][:, :HEAD] - it["ws"][:c]
        for it in cur:
            o = it["ws"][c:] + _mm(it["qk"], it["v_new"])
            o_ref = ob_ref if it["reverse"] else of_ref
            o_ref[0, it["r0"]:it["r0"] + c, it["h"] * HEAD:(it["h"] + 1) * HEAD] = o
        for it in cur:
            st_ref[it["gi"]] = it["state"] * jnp.exp(it["gtot"]) + _mm_tn(it["kd"], it["v_new"])


def _gdn_scan(qkv, gcol, grow, n_heads, *, rows):
    b, s, _ = qkv.shape
    nblk = s // rows
    n_sub = rows // CHUNK
    dg = n_heads * HEAD
    fwd = lambda i, t: (i, t, 0)
    bwd = lambda i, t: (i, nblk - 1 - t, 0)
    out = jax.ShapeDtypeStruct((b, s, dg), F32)
    return pl.pallas_call(
        functools.partial(_gdn_scan_kernel, n_heads=n_heads, n_sub=n_sub),
        out_shape=(out, out),
        grid=(b, nblk),
        in_specs=[pl.BlockSpec((1, rows, 3 * dg), fwd),
                  pl.BlockSpec((1, rows, 3 * dg), bwd),
                  pl.BlockSpec((1, rows, LANES), fwd),
                  pl.BlockSpec((1, rows, LANES), bwd),
                  pl.BlockSpec((1, LANES, rows), lambda i, t: (i, 0, t)),
                  pl.BlockSpec((1, LANES, rows), lambda i, t: (i, 0, nblk - 1 - t))],
        out_specs=(pl.BlockSpec((1, rows, dg), fwd), pl.BlockSpec((1, rows, dg), bwd)),
        scratch_shapes=[pltpu.VMEM((N_DIR * n_heads, HEAD, HEAD), F32)],
        compiler_params=_cparams(("parallel", "arbitrary")),
        name="gdn_scan",
    )(qkv, qkv, gcol, gcol, grow, grow)


def _diff_attn_kernel(lam_ref, q_ref, k_ref, v_ref, o_ref, s_scr, m_scr, *, kc):
    @pl.when(jnp.logical_and(jnp.logical_and(pl.program_id(0) == 0, pl.program_id(1) == 0),
                             pl.program_id(2) == 0))
    def _():
        s_scr[...] = jnp.zeros(s_scr.shape, F32)
        m_scr[...] = jnp.zeros(m_scr.shape, F32)

    for parity in range(2):
        @pl.when(pl.program_id(2) % 2 == parity)
        def _():
            _diff_attn_step(lam_ref, q_ref, k_ref, v_ref, o_ref, s_scr, m_scr,
                            cur=parity, prev=1 - parity, kc=kc)


def _diff_attn_step(lam_ref, q_ref, k_ref, v_ref, o_ref, s_scr, m_scr, *, cur, prev, kc):
    lam = lam_ref[0]
    q = q_ref[0]
    seq = k_ref.shape[1]
    nchunk = seq // kc
    halves = []
    for t in range(2):
        qt = q[:, t * HEAD:(t + 1) * HEAD]
        m_prev = m_scr[prev, t]
        m_lane = None
        l_lane = None
        acc = None
        for c in range(nchunk):
            kt = k_ref[0, c * kc:(c + 1) * kc, t * HEAD:(t + 1) * HEAD]
            s = lax.dot_general(qt, kt, (((1,), (1,)), ((), ())), preferred_element_type=F32)
            s_scr[cur, t, :, c * kc:(c + 1) * kc] = s
            for g in range(kc // LANES):
                sg = s[:, g * LANES:(g + 1) * LANES]
                m_lane = sg if m_lane is None else jnp.maximum(m_lane, sg)
            sp = s_scr[prev, t, :, c * kc:(c + 1) * kc]
            e = jnp.exp(sp - jnp.concatenate([m_prev] * (kc // LANES), axis=1))
            for g in range(kc // LANES):
                eg = e[:, g * LANES:(g + 1) * LANES]
                l_lane = eg if l_lane is None else l_lane + eg
            pv = jnp.dot(e.astype(BF16), v_ref[0, c * kc:(c + 1) * kc, :],
                         preferred_element_type=F32)
            acc = pv if acc is None else acc + pv
        m_scr[cur, t] = jnp.broadcast_to(jnp.max(m_lane, axis=-1, keepdims=True), m_lane.shape)
        halves.append(acc * (1.0 / jnp.sum(l_lane, axis=-1, keepdims=True)))
    o_ref[0] = halves[0] - lam * halves[1]


def _diff_attn(qkv3d, lam, n_heads, *, tq):
    b, s, _ = qkv3d.shape
    dv = 2 * HEAD
    nq = s // tq
    return pl.pallas_call(
        functools.partial(_diff_attn_kernel, kc=_pick(s, ATTN_KC)),
        out_shape=jax.ShapeDtypeStruct((b, s, n_heads * dv), F32),
        grid_spec=pltpu.PrefetchScalarGridSpec(
            num_scalar_prefetch=0,
            grid=(b, n_heads, nq + 1),
            in_specs=[pl.BlockSpec(memory_space=pltpu.SMEM),
                      pl.BlockSpec((1, tq, dv), lambda i, h, t: (i, jnp.minimum(t, nq - 1), h)),
                      pl.BlockSpec((1, s, dv), lambda i, h, t: (i, 0, n_heads + h)),
                      pl.BlockSpec((1, s, dv), lambda i, h, t: (i, 0, 2 * n_heads + h))],
            out_specs=pl.BlockSpec((1, tq, dv), lambda i, h, t: (i, jnp.maximum(t - 1, 0), h)),
            scratch_shapes=[pltpu.VMEM((2, 2, tq, s), F32), pltpu.VMEM((2, 2, tq, LANES), F32)],
        ),
        compiler_params=_cparams(("arbitrary", "arbitrary", "arbitrary")),
        name="diff_attn",
    )(lam, qkv3d, qkv3d, qkv3d)


def _mix_out_kernel(of_ref, ob_ref, z_ref, at_ref, x_ref, gw_ref, sw_ref, wo_ref, n2w_ref, wr_ref,
                    x1_ref, n2_ref, aff_ref, afft_ref, mix_scr, *, n_gdn_heads, n_diff_heads,
                    post_scale, n_experts):
    o = of_ref[...] + ob_ref[...]
    z = z_ref[...]
    gw = gw_ref[...]
    for h in range(n_gdn_heads):
        sl = slice(h * HEAD, (h + 1) * HEAD)
        seg = o[:, sl]
        y = seg * lax.rsqrt(jnp.mean(seg * seg, axis=-1, keepdims=True) + NORM_EPS) * gw
        zz = z[:, sl]
        mix_scr[:, sl] = (y * (zz * _sigmoid(zz))).astype(BF16)
    base = n_gdn_heads * HEAD
    at = at_ref[...]
    sw = sw_ref[...]
    dv = 2 * HEAD
    for h in range(n_diff_heads):
        seg = at[:, h * dv:(h + 1) * dv]
        y = seg * lax.rsqrt(jnp.mean(seg * seg, axis=-1, keepdims=True) + NORM_EPS) * sw
        mix_scr[:, base + h * dv:base + (h + 1) * dv] = (y * post_scale).astype(BF16)
    x1 = x_ref[...] + jnp.dot(mix_scr[...], wo_ref[...], preferred_element_type=F32)
    x1_ref[...] = x1
    n2 = x1 * lax.rsqrt(jnp.mean(x1 * x1, axis=-1, keepdims=True) + NORM_EPS) * n2w_ref[...]
    n_hi = n2.astype(BF16)
    hd = n2.shape[1] // 2
    bits = pltpu.bitcast(n_hi.astype(F32), U32)
    n2_ref[...] = bits[:, :hd] | (bits[:, hd:] >> 16)
    wr = wr_ref[...]
    n_lo = (n2 - n_hi.astype(F32)).astype(BF16)
    w_hi = wr.astype(BF16)
    w_lo = (wr - w_hi.astype(F32)).astype(BF16)
    logits = (jnp.dot(n_hi, w_hi, preferred_element_type=F32)
              + jnp.dot(n_hi, w_lo, preferred_element_type=F32)
              + jnp.dot(n_lo, w_hi, preferred_element_type=F32))
    lane = lax.broadcasted_iota(I32, logits.shape, 1)
    logits = jnp.where(lane < n_experts, logits, -jnp.inf)
    mx = jnp.max(logits, axis=-1, keepdims=True)
    e = jnp.exp(logits - mx)
    aff = e / jnp.sum(e, axis=-1, keepdims=True)
    aff_ref[...] = aff
    afft_ref[0] = aff.T


def _mix_out(o_f, o_b, proj_gdn, attn, x2d, gdn_norm_w, subln_w, w_out_bf16, norm2_w, w_router,
             *, seq, tm, post_scale):
    m, d = x2d.shape
    dg = o_f.shape[1]
    n_gdn_heads = dg // HEAD
    n_diff_heads = attn.shape[1] // (2 * HEAD)
    n_experts = w_router.shape[1]
    nsb = seq // tm
    gw = jnp.tile(gdn_norm_w.reshape(1, HEAD), (1, 1))
    sw = subln_w.reshape(1, 2 * HEAD)
    wr = jnp.zeros((d, LANES), F32).at[:, :n_experts].set(w_router)
    zblk = proj_gdn.shape[1] // dg - 1
    row = lambda i: (i, 0)
    const = lambda i: (0, 0)
    return pl.pallas_call(
        functools.partial(_mix_out_kernel, n_gdn_heads=n_gdn_heads, n_diff_heads=n_diff_heads,
                          post_scale=post_scale, n_experts=n_experts),
        out_shape=(jax.ShapeDtypeStruct((m, d), F32), jax.ShapeDtypeStruct((m, d // 2), U32),
                   jax.ShapeDtypeStruct((m, LANES), F32),
                   jax.ShapeDtypeStruct((m // seq, LANES, seq), F32)),
        grid=(m // tm,),
        in_specs=[pl.BlockSpec((tm, dg), row), pl.BlockSpec((tm, dg), row),
                  pl.BlockSpec((tm, dg), lambda i: (i, zblk)),
                  pl.BlockSpec((tm, attn.shape[1]), row),
                  pl.BlockSpec((tm, d), row),
                  pl.BlockSpec((1, HEAD), const), pl.BlockSpec((1, 2 * HEAD), const),
                  pl.BlockSpec(w_out_bf16.shape, const),
                  pl.BlockSpec((1, d), const), pl.BlockSpec((d, LANES), const)],
        out_specs=(pl.BlockSpec((tm, d), row), pl.BlockSpec((tm, d // 2), row),
                   pl.BlockSpec((tm, LANES), row),
                   pl.BlockSpec((1, LANES, tm), lambda i: (i // nsb, 0, i % nsb))),
        scratch_shapes=[pltpu.VMEM((tm, d), BF16)],
        compiler_params=_cparams(("parallel",)),
        name="mix_out",
    )(o_f, o_b, proj_gdn, attn, x2d, gw, sw, w_out_bf16, norm2_w.reshape(1, d), wr)


def _topk_kernel(afft_ref, idx_ref, gate_ref, post_ref, cs_ref, pos_scr, *, cap):
    x = afft_ref[0]
    ne, s = x.shape
    bits = pltpu.bitcast(x, I32)

    def bisect(i, prefix):
        cand = prefix | (jnp.int32(1) << (30 - i))
        cnt = jnp.sum((bits >= cand).astype(I32), axis=-1, keepdims=True)
        return jnp.where(cnt >= cap, cand, prefix)

    thr = lax.fori_loop(0, 31, bisect, jnp.zeros((ne, 1), I32))
    gt = bits > thr
    eq = bits == thr
    need = cap - jnp.sum(gt.astype(I32), axis=-1, keepdims=True)

    ri = lax.broadcasted_iota(I32, (LANES, LANES), 0)
    ci = lax.broadcasted_iota(I32, (LANES, LANES), 1)
    upper_strict = (ri < ci).astype(BF16)

    def excl_prefix(mask):
        carry = jnp.zeros((ne, 1), F32)
        parts, starts = [], []
        for blk in range(s // LANES):
            mb = mask[:, blk * LANES:(blk + 1) * LANES].astype(BF16)
            parts.append(jnp.dot(mb, upper_strict, preferred_element_type=F32) + carry)
            starts.append(carry)
            carry = carry + jnp.sum(mb.astype(F32), axis=-1, keepdims=True)
        return jnp.concatenate(parts, axis=1), jnp.concatenate(starts, axis=1)

    eq_rank, _ = excl_prefix(eq)
    sel = jnp.logical_or(gt, jnp.logical_and(eq, eq_rank < need.astype(F32)))
    pos, starts = excl_prefix(sel)

    posm = jnp.where(sel, pos, -1.0)
    pos_scr[...] = posm
    cs_ref[0] = starts.astype(I32)
    posm_pad = jnp.concatenate([posm, jnp.full((LANES - ne, s), -1.0, F32)], axis=0)
    for blk in range(s // LANES):
        post_ref[0, blk * LANES:(blk + 1) * LANES, :] = posm_pad[:, blk * LANES:(blk + 1) * LANES].T

    tb = min(s, 512)
    tok = lax.broadcasted_iota(I32, (1, tb), 1)
    slot = lax.broadcasted_iota(I32, (cap, tb), 0).astype(F32)
    zero = jnp.zeros((3, tb), F32)

    def per_expert(e, carry):
        r = jnp.zeros((8, cap), F32)
        for blk in range(s // tb):
            prow = pos_scr[pl.ds(e, 1), blk * tb:(blk + 1) * tb]
            onehot = (prow == slot).astype(BF16)
            a = afft_ref[0, pl.ds(e, 1), blk * tb:(blk + 1) * tb]
            a_hi = a.astype(BF16).astype(F32)
            a_mid = (a - a_hi).astype(BF16).astype(F32)
            a_lo = a - a_hi - a_mid
            t = tok + blk * tb
            lhs = jnp.concatenate([(t // 64).astype(F32), (t % 64).astype(F32),
                                   a_hi, a_mid, a_lo, zero], axis=0)
            r = r + _mm_nt(lhs, onehot)
        idx_ref[0, pl.ds(e, 1), :] = (r[0:1] * 64.0 + r[1:2]).astype(I32)
        gate_ref[0, pl.ds(e, 1), :] = r[2:3] + r[3:4] + r[4:5]
        return carry

    lax.fori_loop(0, ne, per_expert, 0)


def _topk(afft, n_experts, cap):
    b, _, s = afft.shape
    return pl.pallas_call(
        functools.partial(_topk_kernel, cap=cap),
        out_shape=(jax.ShapeDtypeStruct((b, n_experts, cap), I32),
                   jax.ShapeDtypeStruct((b, n_experts, cap), F32),
                   jax.ShapeDtypeStruct((b, s, LANES), F32),
                   jax.ShapeDtypeStruct((b, n_experts, s // LANES), I32)),
        grid=(b,),
        in_specs=[pl.BlockSpec((1, n_experts, s), lambda i: (i, 0, 0))],
        out_specs=(pl.BlockSpec((1, n_experts, cap), lambda i: (i, 0, 0)),
                   pl.BlockSpec((1, n_experts, cap), lambda i: (i, 0, 0)),
                   pl.BlockSpec((1, s, LANES), lambda i: (i, 0, 0)),
                   pl.BlockSpec((1, n_experts, s // LANES), lambda i: (i, 0, 0))),
        scratch_shapes=[pltpu.VMEM((n_experts, s), F32)],
        compiler_params=_cparams(("parallel",)),
        name="ec_topk",
    )(afft)


def _ffn_kernel(rows_ref, src_ref, wg_ref, wu_ref, wd_ref, gate_ref, o_ref,
                acc_scr, gbuf, x_scr, sem, *, tmr, n_f, n_blocks):
    f = pl.program_id(2)
    blk = pl.program_id(0) * pl.num_programs(1) + pl.program_id(1)
    step = blk * n_f + f
    per_step = tmr // n_f
    unroll = 8

    def row_copy(b, i):
        return pltpu.make_async_copy(src_ref.at[pl.ds(rows_ref[b * tmr + i], 1), :],
                                     gbuf.at[pl.ds(i, 1), :], sem)

    def wait_block(b):
        def body(i, carry):
            for u in range(unroll):
                row_copy(b, i * unroll + u).wait()
            return carry
        lax.fori_loop(0, tmr // unroll, body, 0)

    @pl.when(step == 0)
    def _():
        acc_scr[...] = jnp.zeros(acc_scr.shape, F32)

        def body(i, carry):
            for u in range(unroll):
                row_copy(0, i * unroll + u).start()
            return carry
        lax.fori_loop(0, tmr // unroll, body, 0)

    @pl.when(f == 0)
    def _():
        wait_block(blk)
        words = gbuf[...]
        hd = words.shape[1]
        x_scr[:, :hd] = pltpu.bitcast(words & jnp.uint32(0xFFFF0000), F32).astype(BF16)
        x_scr[:, hd:] = pltpu.bitcast(words << 16, F32).astype(BF16)

    nxt = jnp.minimum(blk + 1, n_blocks - 1)
    for u in range(per_step):
        row_copy(nxt, f * per_step + u).start()

    x = x_scr[...]
    g = jnp.dot(x, wg_ref[0, 0].astype(BF16), preferred_element_type=F32)
    u = jnp.dot(x, wu_ref[0, 0].astype(BF16), preferred_element_type=F32)
    hid = (g * _sigmoid(g) * u).astype(BF16)
    part = jnp.dot(hid, wd_ref[0, 0].astype(BF16), preferred_element_type=F32)
    acc = jnp.where(f == 0, part, acc_scr[...] + part)
    acc_scr[...] = acc
    o_ref[0] = acc * gate_ref[0]

    @pl.when(step == n_blocks * n_f - 1)
    def _():
        wait_block(nxt)


def _expert_ffn(rows_flat, packed, w_gate, w_up, w_down, gate_col, *, layer, rows, tmr, tf):
    ne, d, dff = w_gate.shape[1], w_gate.shape[2], w_gate.shape[3]
    n_r = rows // tmr
    n_f = dff // tf
    return pl.pallas_call(
        functools.partial(_ffn_kernel, tmr=tmr, n_f=n_f, n_blocks=ne * n_r),
        out_shape=jax.ShapeDtypeStruct((ne, rows, d), F32),
        grid_spec=pltpu.PrefetchScalarGridSpec(
            num_scalar_prefetch=1,
            grid=(ne, n_r, n_f),
            in_specs=[pl.BlockSpec(memory_space=pl.ANY),
                      pl.BlockSpec((1, 1, d, tf), lambda e, r, f, idx: (layer, e, 0, f)),
                      pl.BlockSpec((1, 1, d, tf), lambda e, r, f, idx: (layer, e, 0, f)),
                      pl.BlockSpec((1, 1, tf, d), lambda e, r, f, idx: (layer, e, f, 0)),
                      pl.BlockSpec((1, tmr, 1), lambda e, r, f, idx: (e, r, 0))],
            out_specs=pl.BlockSpec((1, tmr, d), lambda e, r, f, idx: (e, r, 0)),
            scratch_shapes=[pltpu.VMEM((tmr, d), F32), pltpu.VMEM((tmr, d // 2), U32),
                            pltpu.VMEM((tmr, d), BF16), pltpu.SemaphoreType.DMA(())],
        ),
        compiler_params=_cparams(("arbitrary", "arbitrary", "arbitrary")),
        name="ec_ffn",
    )(rows_flat, packed, w_gate, w_up, w_down, gate_col)


def _combine_kernel(cs_ref, nr_ref, pos_ref, x1_ref, fw_ref, y_hbm, o_ref, slab, sem, *,
                    w, cap, ne, nblk, final_norm):
    j = pl.program_id(1)
    step = pl.program_id(0) * nblk + j
    nsteps = pl.num_programs(0) * nblk
    cur = step % 2
    per = LANES // w

    def window(st, e, r):
        first = cs_ref[st * ne + e] + r * (w - SUBLANES)
        start = jnp.minimum((first // SUBLANES) * SUBLANES, cap - w)
        return first, pl.multiple_of(start, SUBLANES)

    def copy(st, e, r, buf):
        _, start = window(st, e, r)
        row = pl.multiple_of((st // nblk) * cap + start, SUBLANES)
        return pltpu.make_async_copy(y_hbm.at[e, pl.ds(row, w), :],
                                     slab.at[buf, pl.ds(e * w, w), :], sem.at[buf])

    def fetch(st, r, buf):
        for e in range(ne):
            copy(st, e, r, buf).start()

    def drain(st, r, buf):
        for e in range(ne):
            copy(st, e, r, buf).wait()

    @pl.when(step == 0)
    def _():
        fetch(0, 0, 0)

    drain(step, 0, cur)

    @pl.when(step + 1 < nsteps)
    def _():
        fetch(step + 1, 0, 1 - cur)

    pos = pos_ref[0]
    lane = lax.broadcasted_iota(I32, pos.shape, 1)
    lane_f = lane.astype(F32)

    def contrib(r, buf):
        cols = []
        for grp in range(ne // per):
            e = grp * per + per - 1
            first, start = window(step, e, r)
            pe = pos[:, e:e + 1]
            first_v = first.astype(F32)
            base_v = (start - (per - 1) * w).astype(F32)
            for q in range(per - 2, -1, -1):
                e = grp * per + q
                first, start = window(step, e, r)
                m = lane < (q + 1) * w
                pe = jnp.where(m, pos[:, e:e + 1], pe)
                first_v = jnp.where(m, first.astype(F32), first_v)
                base_v = jnp.where(m, (start - q * w).astype(F32), base_v)
            hit = jnp.logical_and(jnp.logical_and(pe >= first_v, pe < first_v + (w - SUBLANES)),
                                  pe - base_v == lane_f)
            cols.append(hit.astype(BF16))
        sel = jnp.concatenate(cols, axis=1)
        y = slab[buf]
        y_hi = y.astype(BF16)
        y_lo = (y - y_hi.astype(F32)).astype(BF16)
        return (jnp.dot(sel, y_hi, preferred_element_type=F32)
                + jnp.dot(sel, y_lo, preferred_element_type=F32))

    acc = x1_ref[...] + contrib(0, cur)

    def extra_round(r, acc):
        fetch(step, r, cur)
        drain(step, r, cur)
        return acc + contrib(r, cur)

    acc = lax.fori_loop(1, nr_ref[step], extra_round, acc)
    if final_norm:
        acc = acc * lax.rsqrt(jnp.mean(acc * acc, axis=-1, keepdims=True) + NORM_EPS) * fw_ref[...]
    o_ref[...] = acc


def _combine(cs_flat, nr_flat, post, x1, y, final_w, *, seq, tb, w, cap, final_norm):
    m, d = x1.shape
    ne = y.shape[0]
    nblk = seq // tb
    return pl.pallas_call(
        functools.partial(_combine_kernel, w=w, cap=cap, ne=ne, nblk=nblk, final_norm=final_norm),
        out_shape=jax.ShapeDtypeStruct((m, d), F32),
        grid_spec=pltpu.PrefetchScalarGridSpec(
            num_scalar_prefetch=2,
            grid=(m // seq, nblk),
            in_specs=[pl.BlockSpec((1, tb, LANES), lambda b, j, cs, nr: (b, j, 0)),
                      pl.BlockSpec((tb, d), lambda b, j, cs, nr: (b * nblk + j, 0)),
                      pl.BlockSpec((1, d), lambda b, j, cs, nr: (0, 0)),
                      pl.BlockSpec(memory_space=pl.ANY)],
            out_specs=pl.BlockSpec((tb, d), lambda b, j, cs, nr: (b * nblk + j, 0)),
            scratch_shapes=[pltpu.VMEM((2, ne * w, d), F32), pltpu.SemaphoreType.DMA((2,))],
        ),
        compiler_params=_cparams(("arbitrary", "arbitrary")),
        name="ec_combine",
    )(cs_flat, nr_flat, post, x1, final_w.reshape(1, d), y)


def _pick(n, pref):
    t = min(n, pref)
    while n % t:
        t //= 2
    return t


def kernel(x, norm1_w, w_in, conv_w, a_log, dt_bias, gdn_norm_w, diff_lambda, diff_subln_w, w_out,
           norm2_w, w_router, w_gate, w_up, w_down, final_norm_w):
    bsz, seq, d = x.shape
    depth = w_in.shape[0]
    d_gdn = d // 2
    d_diff = d - d_gdn
    n_gdn_heads = d_gdn // HEAD
    n_diff_heads = d_diff // (2 * HEAD)
    n_gate = N_DIR * n_gdn_heads
    n_experts = w_router.shape[2]
    cap = EC_CAPACITY * seq // n_experts
    m = bsz * seq
    c_gate = 4 * d_gdn
    c_diff = c_gate + 2 * n_gate

    half = HEAD // 2
    inv_freq = ROPE_THETA ** (-jnp.arange(half, dtype=F32) / half)
    ang = jnp.arange(seq, dtype=F32)[:, None] * inv_freq[None, :]
    cos_t = jnp.concatenate([jnp.cos(ang), jnp.cos(ang)], axis=1)
    sin_t = jnp.concatenate([-jnp.sin(ang), jnp.sin(ang)], axis=1)

    tm_proj = _pick(seq, 1024)
    x2 = x.reshape(m, d)
    for l in range(depth):
        w_l = w_in[l]
        w_gdn = w_l[:, :c_gate].astype(BF16)
        w_gt = jnp.zeros((d, LANES), BF16).at[:, :2 * n_gate].set(w_l[:, c_gate:c_diff].astype(BF16))
        w_df = w_l[:, c_diff:].astype(BF16)

        proj_gdn = _norm_proj(x2, norm1_w[l], w_gdn, tm=tm_proj, tn=PROJ_TN, out_dtype=F32)
        gates = _norm_proj(x2, norm1_w[l], w_gt, tm=tm_proj, tn=LANES, out_dtype=F32)
        n_blk = d_diff // PROJ_TN
        proj_diff = _norm_proj(x2, norm1_w[l], w_df, tm=tm_proj, tn=PROJ_TN, out_dtype=BF16,
                               rot=(cos_t, sin_t, seq, n_blk, 2 * n_blk))

        qkv = _gdn_prep(proj_gdn.reshape(bsz, seq, 4 * d_gdn), conv_w[l], n_gdn_heads)
        gcol, grow = _gate_prep(gates, a_log[l], dt_bias[l], seq=seq, tm=_pick(seq, 512))
        o_f, o_b = _gdn_scan(qkv, gcol.reshape(bsz, seq, LANES), grow, n_gdn_heads, rows=_pick(seq, SCAN_ROWS))

        lf = diff_lambda[l].astype(F32)
        lambda_init = 0.8 - 0.6 * math.exp(-0.3 * l)
        lam = (jnp.exp(jnp.sum(lf[0] * lf[1])) - jnp.exp(jnp.sum(lf[2] * lf[3])) + lambda_init)
        attn = _diff_attn(proj_diff.reshape(bsz, seq, 3 * d_diff), lam.reshape(1).astype(F32),
                          n_diff_heads, tq=_pick(seq, ATTN_TQ))

        x1, n2, _, afft = _mix_out(
            o_f.reshape(m, d_gdn), o_b.reshape(m, d_gdn), proj_gdn, attn.reshape(m, d_diff), x2,
            gdn_norm_w[l], diff_subln_w[l], w_out[l].astype(BF16), norm2_w[l], w_router[l],
            seq=seq, tm=_pick(seq, 256), post_scale=1.0 - lambda_init)

        idx, gate, post, cs128 = _topk(afft, n_experts, cap)
        rows = (idx + (jnp.arange(bsz, dtype=I32) * seq)[:, None, None]).transpose(1, 0, 2)
        gate_col = gate.transpose(1, 0, 2).reshape(n_experts, bsz * cap, 1)
        y = _expert_ffn(rows.reshape(-1), n2, w_gate, w_up, w_down, gate_col, layer=l,
                        rows=bsz * cap, tmr=_pick(bsz * cap, FFN_TM), tf=FFN_TF)
        tb = _pick(seq, COMBINE_TB)
        win = min(COMBINE_W, cap)
        cs = cs128[:, :, ::tb // LANES].transpose(0, 2, 1)
        run = jnp.concatenate([cs[:, 1:], jnp.full((bsz, 1, n_experts), cap, I32)], axis=1) - cs
        per_round = win - SUBLANES
        rounds = jnp.maximum(1, jnp.max((run + per_round - 1) // per_round, axis=-1))
        x2 = _combine(cs.reshape(-1), rounds.reshape(-1).astype(I32), post, x1, y, final_norm_w,
                      seq=seq, tb=tb, w=win, cap=cap, final_norm=l == depth - 1)
    return x2.reshape(bsz, seq, d)
```

```python
import functools
import math

import jax
import jax.numpy as jnp
from jax import lax
from jax.experimental import pallas as pl
from jax.experimental.pallas import tpu as pltpu

F32 = jnp.float32
BF16 = jnp.bfloat16
I32 = jnp.int32
U32 = jnp.uint32

NORM_EPS = 1e-6
LANES = 128
SUBLANES = 8
HEAD = 128
GDN_CONV = 5
CHUNK = 64
N_DIR = 2
N_EXPERTS = 16
EC_CAPACITY = 2
ROPE_THETA = 10000.0
VMEM_LIMIT = 56 * 1024 * 1024
HI = lax.Precision.HIGHEST
SCAN_ROWS = 256
PREP_ROWS = 512
ATTN_TQ = 256
ATTN_KC = 512
PROJ_TN = 1024
FFN_TM = 1024
FFN_TF = 256
COMBINE_TB = 256
COMBINE_W = 64


def _cparams(sem):
    return pltpu.CompilerParams(dimension_semantics=sem, vmem_limit_bytes=VMEM_LIMIT)


def _sigmoid(x):
    return 1.0 / (1.0 + jnp.exp(-x))


def _mm(a, b):
    return jnp.dot(a.astype(BF16), b.astype(BF16), preferred_element_type=F32)


def _mm_nt(a, b):
    return lax.dot_general(a.astype(BF16), b.astype(BF16), (((1,), (1,)), ((), ())),
                           preferred_element_type=F32)


def _mm_tn(a, b):
    return lax.dot_general(a.astype(BF16), b.astype(BF16), (((0,), (0,)), ((), ())),
                           preferred_element_type=F32)


def _norm_rows(x_ref, nw_ref, n_scr):
    x = x_ref[...]
    ms = jnp.mean(x * x, axis=-1, keepdims=True)
    n_scr[...] = (x * lax.rsqrt(ms + NORM_EPS) * nw_ref[...]).astype(BF16)


def _norm_proj_kernel(x_ref, nw_ref, w_ref, o_ref, n_scr):
    @pl.when(pl.program_id(1) == 0)
    def _():
        _norm_rows(x_ref, nw_ref, n_scr)

    o_ref[...] = jnp.dot(n_scr[...], w_ref[...], preferred_element_type=F32).astype(o_ref.dtype)


def _norm_proj_rot_kernel(x_ref, nw_ref, w_ref, cos_ref, sin_ref, o_ref, n_scr, *,
                          n_q_blocks, n_rot_blocks):
    j = pl.program_id(1)

    @pl.when(j == 0)
    def _():
        _norm_rows(x_ref, nw_ref, n_scr)

    acc = jnp.dot(n_scr[...], w_ref[...], preferred_element_type=F32)
    tn = acc.shape[1]
    is_rot = j < n_rot_blocks
    scale = jnp.where(j < n_q_blocks, HEAD ** -0.5, 1.0).astype(F32)
    c = cos_ref[...]
    s = sin_ref[...]
    for g in range(tn // HEAD):
        t = acc[:, g * HEAD:(g + 1) * HEAD]
        r = t * c + pltpu.roll(t, HEAD // 2, axis=1) * s
        o_ref[:, g * HEAD:(g + 1) * HEAD] = (jnp.where(is_rot, r, t) * scale).astype(o_ref.dtype)


def _norm_proj(x2d, nw, w_bf16, *, tm, tn, out_dtype, rot=None):
    m, d = x2d.shape
    n = w_bf16.shape[1]
    grid = (m // tm, n // tn)
    in_specs = [pl.BlockSpec((tm, d), lambda i, j: (i, 0)),
                pl.BlockSpec((1, d), lambda i, j: (0, 0)),
                pl.BlockSpec((d, tn), lambda i, j: (0, j))]
    args = [x2d, nw.reshape(1, d), w_bf16]
    if rot is None:
        kern = _norm_proj_kernel
    else:
        cos_t, sin_t, seq, n_q_blocks, n_rot_blocks = rot
        nsb = seq // tm
        in_specs += [pl.BlockSpec((tm, HEAD), lambda i, j: (i % nsb, 0)),
                     pl.BlockSpec((tm, HEAD), lambda i, j: (i % nsb, 0))]
        args += [cos_t, sin_t]
        kern = functools.partial(_norm_proj_rot_kernel, n_q_blocks=n_q_blocks,
                                 n_rot_blocks=n_rot_blocks)
    return pl.pallas_call(
        kern,
        out_shape=jax.ShapeDtypeStruct((m, n), out_dtype),
        grid=grid,
        in_specs=in_specs,
        out_specs=pl.BlockSpec((tm, tn), lambda i, j: (i, j)),
        scratch_shapes=[pltpu.VMEM((tm, d), BF16)],
        compiler_params=_cparams(("parallel", "arbitrary")),
        name="norm_proj" if rot is None else "norm_proj_rot",
    )(*args)


def _gdn_prep_kernel(x_ref, w_ref, o_ref, *, n_heads):
    j = pl.program_id(1)
    w = w_ref[...]
    s = x_ref.shape[1]
    rows = min(s, PREP_ROWS)
    half = GDN_CONV // 2
    row = lax.broadcasted_iota(I32, (rows, HEAD), 0)
    is_q = j < n_heads
    is_qk = j < 2 * n_heads
    for r0 in range(0, s, rows):
        acc = x_ref[0, r0:r0 + rows, :] * w[half:half + 1, :]
        for k in range(GDN_CONV):
            off = k - half
            if off == 0:
                continue
            if 0 <= r0 + off and r0 + off + rows <= s:
                xs = x_ref[0, r0 + off:r0 + off + rows, :]
            else:
                xs = pltpu.roll(x_ref[0, r0:r0 + rows, :], (-off) % rows, axis=0)
                valid = jnp.logical_and(row + (r0 + off) >= 0, row + (r0 + off) < s)
                xs = jnp.where(valid, xs, 0.0)
            acc = acc + xs * w[k:k + 1, :]
        y = acc * _sigmoid(acc)
        ss = jnp.sum(y * y, axis=-1, keepdims=True)
        nrm = y * lax.rsqrt(ss + NORM_EPS)
        out = jnp.where(is_q, nrm * (HEAD ** -0.5), jnp.where(is_qk, nrm, y))
        o_ref[0, r0:r0 + rows, :] = out


def _gdn_prep(proj3d, conv_w, n_heads):
    b, s, _ = proj3d.shape
    ncol = 3 * n_heads
    return pl.pallas_call(
        functools.partial(_gdn_prep_kernel, n_heads=n_heads),
        out_shape=jax.ShapeDtypeStruct((b, s, ncol * HEAD), F32),
        grid=(b, ncol),
        in_specs=[pl.BlockSpec((1, s, HEAD), lambda i, j: (i, 0, j)),
                  pl.BlockSpec((GDN_CONV, HEAD), lambda i, j: (0, j))],
        out_specs=pl.BlockSpec((1, s, HEAD), lambda i, j: (i, 0, j)),
        compiler_params=_cparams(("parallel", "parallel")),
        name="gdn_prep",
    )(proj3d, conv_w)


def _gate_prep_kernel(x_ref, alog_ref, dtb_ref, col_ref, row_ref, *, n_gate):
    x = x_ref[...]
    lane = lax.broadcasted_iota(I32, x.shape, 1)
    beta = _sigmoid(x)
    a = x + dtb_ref[...]
    sp = jnp.maximum(a, 0.0) + jnp.log(1.0 + jnp.exp(-jnp.abs(a)))
    g = -jnp.exp(alog_ref[...]) * sp
    y = jnp.where(lane < n_gate, beta, jnp.where(lane < 2 * n_gate, g, 0.0))
    col_ref[...] = y
    row_ref[0] = y.T


def _gate_prep(gates2d, a_log, dt_bias, *, seq, tm):
    m = gates2d.shape[0]
    n_gate = a_log.size
    nsb = seq // tm
    pad = lambda v: jnp.zeros((1, LANES), F32).at[0, n_gate:2 * n_gate].set(v.reshape(-1).astype(F32))
    return pl.pallas_call(
        functools.partial(_gate_prep_kernel, n_gate=n_gate),
        out_shape=(jax.ShapeDtypeStruct((m, LANES), F32),
                   jax.ShapeDtypeStruct((m // seq, LANES, seq), F32)),
        grid=(m // tm,),
        in_specs=[pl.BlockSpec((tm, LANES), lambda i: (i, 0)),
                  pl.BlockSpec((1, LANES), lambda i: (0, 0)),
                  pl.BlockSpec((1, LANES), lambda i: (0, 0))],
        out_specs=(pl.BlockSpec((tm, LANES), lambda i: (i, 0)),
                   pl.BlockSpec((1, LANES, tm), lambda i: (i // nsb, 0, i % nsb))),
        compiler_params=_cparams(("parallel",)),
        name="gate_prep",
    )(gates2d, pad(a_log), pad(dt_bias))


def _gdn_scan_kernel(qkv_f_ref, qkv_b_ref, gcol_f_ref, gcol_b_ref, grow_f_ref, grow_b_ref,
                     of_ref, ob_ref, st_ref, *, n_heads, n_sub):
    @pl.when(pl.program_id(1) == 0)
    def _():
        st_ref[...] = jnp.zeros(st_ref.shape, F32)

    c = CHUNK
    ri = lax.broadcasted_iota(I32, (c, c), 0)
    ci = lax.broadcasted_iota(I32, (c, c), 1)
    eye = (ri == ci).astype(F32)
    strict = {False: ri > ci, True: ri < ci}
    incl = {False: ri >= ci, True: ri <= ci}
    lower_incl = (ri >= ci).astype(F32)
    upper_incl = (ri <= ci).astype(F32)
    ng = N_DIR * n_heads

    items = []
    for d in range(N_DIR):
        reverse = d == 1
        qkv_ref = qkv_b_ref if reverse else qkv_f_ref
        gcol_ref = gcol_b_ref if reverse else gcol_f_ref
        grow_ref = grow_b_ref if reverse else grow_f_ref
        for sub in range(n_sub):
            r0 = sub * c
            gcol = gcol_ref[0, r0:r0 + c, :]
            grow = grow_ref[0, ng:2 * ng, r0:r0 + c]
            gc_col = jnp.dot(upper_incl if reverse else lower_incl, gcol,
                             precision=HI, preferred_element_type=F32)
            gc_row = jnp.dot(grow, lower_incl if reverse else upper_incl,
                             precision=HI, preferred_element_type=F32)
            last = 0 if reverse else c - 1
            for h in range(n_heads):
                gi = d * n_heads + h
                it = dict(d=d, sub=sub, h=h, gi=gi, r0=r0, reverse=reverse)
                it["q"] = qkv_ref[0, r0:r0 + c, h * HEAD:(h + 1) * HEAD]
                it["k"] = qkv_ref[0, r0:r0 + c, (n_heads + h) * HEAD:(n_heads + h + 1) * HEAD]
                it["v"] = qkv_ref[0, r0:r0 + c, (2 * n_heads + h) * HEAD:(2 * n_heads + h + 1) * HEAD]
                it["beta"] = gcol[:, gi:gi + 1]
                it["gc_c"] = gc_col[:, ng + gi:ng + gi + 1]
                it["gc_r"] = gc_row[gi:gi + 1, :]
                it["gtot"] = it["gc_r"][:, last:last + 1]
                items.append(it)

    for it in items:
        m_incl = incl[it["reverse"]]
        gdiff = it["gc_c"] - it["gc_r"]
        it["decay"] = jnp.where(m_incl, jnp.exp(jnp.where(m_incl, gdiff, 0.0)), 0.0)
        it["kb"] = it["k"] * it["beta"]
    for it in items:
        it["aq"] = _mm_nt(jnp.concatenate([it["kb"], it["q"]], axis=0), it["k"])
    for it in items:
        low = jnp.where(strict[it["reverse"]], it["aq"][:c] * it["decay"], 0.0)
        it["qk"] = it["aq"][c:] * it["decay"]
        it["p"] = low
        it["inv"] = eye - low
    for _ in range(int(math.log2(c)) - 1):
        for it in items:
            it["p"] = _mm(it["p"], it["p"])
        for it in items:
            it["inv"] = it["inv"] + _mm(it["inv"], it["p"])
    for it in items:
        eg = jnp.exp(it["gc_c"])
        rhs = jnp.concatenate([it["v"] * it["beta"], it["kb"] * eg], axis=1)
        it["qg"] = it["q"] * eg
        it["kd"] = it["k"] * jnp.exp(it["gtot"] - it["gc_c"])
        it["sol"] = _mm(it["inv"], rhs)

    for step in range(n_sub):
        cur = [it for it in items if it["sub"] == (n_sub - 1 - step if it["reverse"] else step)]
        for it in cur:
            it["state"] = st_ref[it["gi"]]
            w = it["sol"][:, HEAD:]
            it["ws"] = _mm(jnp.concatenate([w, it["qg"]], axis=0), it["state"])
        for it in cur:
            it["v_new"] = it["sol"][:, :HEAD] - it["ws"][:c]
        for it in cur:
            o = it["ws"][c:] + _mm(it["qk"], it["v_new"])
            o_ref = ob_ref if it["reverse"] else of_ref
            o_ref[0, it["r0"]:it["r0"] + c, it["h"] * HEAD:(it["h"] + 1) * HEAD] = o
        for it in cur:
            st_ref[it["gi"]] = it["state"] * jnp.exp(it["gtot"]) + _mm_tn(it["kd"], it["v_new"])


def _gdn_scan(qkv, gcol, grow, n_heads, *, rows):
    b, s, _ = qkv.shape
    nblk = s // rows
    n_sub = rows // CHUNK
    dg = n_heads * HEAD
    fwd = lambda i, t: (i, t, 0)
    bwd = lambda i, t: (i, nblk - 1 - t, 0)
    out = jax.ShapeDtypeStruct((b, s, dg), F32)
    return pl.pallas_call(
        functools.partial(_gdn_scan_kernel, n_heads=n_heads, n_sub=n_sub),
        out_shape=(out, out),
        grid=(b, nblk),
        in_specs=[pl.BlockSpec((1, rows, 3 * dg), fwd),
                  pl.BlockSpec((1, rows, 3 * dg), bwd),
                  pl.BlockSpec((1, rows, LANES), fwd),
                  pl.BlockSpec((1, rows, LANES), bwd),
                  pl.BlockSpec((1, LANES, rows), lambda i, t: (i, 0, t)),
                  pl.BlockSpec((1, LANES, rows), lambda i, t: (i, 0, nblk - 1 - t))],
        out_specs=(pl.BlockSpec((1, rows, dg), fwd), pl.BlockSpec((1, rows, dg), bwd)),
        scratch_shapes=[pltpu.VMEM((N_DIR * n_heads, HEAD, HEAD), F32)],
        compiler_params=_cparams(("parallel", "arbitrary")),
        name="gdn_scan",
    )(qkv, qkv, gcol, gcol, grow, grow)


def _diff_attn_kernel(lam_ref, q_ref, k_ref, v_ref, o_ref, s_scr, m_scr, *, kc):
    @pl.when(pl.program_id(0) == 0)
    def _():
        s_scr[...] = jnp.zeros(s_scr.shape, F32)
        m_scr[...] = jnp.zeros(m_scr.shape, F32)

    for parity in range(2):
        @pl.when(pl.program_id(0) % 2 == parity)
        def _():
            _diff_attn_step(lam_ref, q_ref, k_ref, v_ref, o_ref, s_scr, m_scr,
                            cur=parity, prev=1 - parity, kc=kc)


def _diff_attn_step(lam_ref, q_ref, k_ref, v_ref, o_ref, s_scr, m_scr, *, cur, prev, kc):
    lam = lam_ref[0]
    q = q_ref[0]
    seq = k_ref.shape[1]
    nchunk = seq // kc
    halves = []
    for t in range(2):
        qt = q[:, t * HEAD:(t + 1) * HEAD]
        m_prev = m_scr[prev, t]
        m_lane = None
        l_lane = None
        acc = None
        for c in range(nchunk):
            kt = k_ref[0, c * kc:(c + 1) * kc, t * HEAD:(t + 1) * HEAD]
            s = lax.dot_general(qt, kt, (((1,), (1,)), ((), ())), preferred_element_type=F32)
            s_scr[cur, t, :, c * kc:(c + 1) * kc] = s
            for g in range(kc // LANES):
                sg = s[:, g * LANES:(g + 1) * LANES]
                m_lane = sg if m_lane is None else jnp.maximum(m_lane, sg)
            sp = s_scr[prev, t, :, c * kc:(c + 1) * kc]
            e = jnp.exp(sp - jnp.concatenate([m_prev] * (kc // LANES), axis=1))
            for g in range(kc // LANES):
                eg = e[:, g * LANES:(g + 1) * LANES]
                l_lane = eg if l_lane is None else l_lane + eg
            pv = jnp.dot(e.astype(BF16), v_ref[0, c * kc:(c + 1) * kc, :],
                         preferred_element_type=F32)
            acc = pv if acc is None else acc + pv
        m_scr[cur, t] = jnp.broadcast_to(jnp.max(m_lane, axis=-1, keepdims=True), m_lane.shape)
        halves.append(acc * (1.0 / jnp.sum(l_lane, axis=-1, keepdims=True)))
    o_ref[0] = halves[0] - lam * halves[1]


def _diff_attn(qkv3d, lam, n_heads, *, tq):
    b, s, _ = qkv3d.shape
    dv = 2 * HEAD
    nq = s // tq
    total = b * n_heads * nq

    def split(step):
        step = jnp.clip(step, 0, total - 1)
        return step // (n_heads * nq), (step // nq) % n_heads, step % nq

    def q_map(i):
        bi, h, t = split(i)
        return bi, t, h

    def k_map(i):
        bi, h, _ = split(i)
        return bi, 0, n_heads + h

    def v_map(i):
        bi, h, _ = split(i - 1)
        return bi, 0, 2 * n_heads + h

    def o_map(i):
        bi, h, t = split(i - 1)
        return bi, t, h

    return pl.pallas_call(
        functools.partial(_diff_attn_kernel, kc=_pick(s, ATTN_KC)),
        out_shape=jax.ShapeDtypeStruct((b, s, n_heads * dv), F32),
        grid_spec=pltpu.PrefetchScalarGridSpec(
            num_scalar_prefetch=0,
            grid=(total + 1,),
            in_specs=[pl.BlockSpec(memory_space=pltpu.SMEM),
                      pl.BlockSpec((1, tq, dv), q_map),
                      pl.BlockSpec((1, s, dv), k_map),
                      pl.BlockSpec((1, s, dv), v_map)],
            out_specs=pl.BlockSpec((1, tq, dv), o_map),
            scratch_shapes=[pltpu.VMEM((2, 2, tq, s), F32), pltpu.VMEM((2, 2, tq, LANES), F32)],
        ),
        compiler_params=_cparams(("arbitrary",)),
        name="diff_attn",
    )(lam, qkv3d, qkv3d, qkv3d)


def _mix_out_kernel(of_ref, ob_ref, z_ref, at_ref, x_ref, gw_ref, sw_ref, wo_ref, n2w_ref, wr_ref,
                    x1_ref, n2_ref, aff_ref, afft_ref, mix_scr, *, n_gdn_heads, n_diff_heads,
                    post_scale, n_experts):
    o = of_ref[...] + ob_ref[...]
    z = z_ref[...]
    gw = gw_ref[...]
    for h in range(n_gdn_heads):
        sl = slice(h * HEAD, (h + 1) * HEAD)
        seg = o[:, sl]
        y = seg * lax.rsqrt(jnp.mean(seg * seg, axis=-1, keepdims=True) + NORM_EPS) * gw
        zz = z[:, sl]
        mix_scr[:, sl] = (y * (zz * _sigmoid(zz))).astype(BF16)
    base = n_gdn_heads * HEAD
    at = at_ref[...]
    sw = sw_ref[...]
    dv = 2 * HEAD
    for h in range(n_diff_heads):
        seg = at[:, h * dv:(h + 1) * dv]
        y = seg * lax.rsqrt(jnp.mean(seg * seg, axis=-1, keepdims=True) + NORM_EPS) * sw
        mix_scr[:, base + h * dv:base + (h + 1) * dv] = (y * post_scale).astype(BF16)
    x1 = x_ref[...] + jnp.dot(mix_scr[...], wo_ref[...], preferred_element_type=F32)
    x1_ref[...] = x1
    n2 = x1 * lax.rsqrt(jnp.mean(x1 * x1, axis=-1, keepdims=True) + NORM_EPS) * n2w_ref[...]
    n_hi = n2.astype(BF16)
    hd = n2.shape[1] // 2
    bits = pltpu.bitcast(n_hi.astype(F32), U32)
    words = bits[:, :hd] | (bits[:, hd:] >> 16)
    chunks = jnp.stack([words[:, k * LANES:(k + 1) * LANES] for k in range(hd // LANES)], axis=0)
    n2_ref[...] = pltpu.einshape("kmd->mkd", chunks)
    wr = wr_ref[...]
    n_lo = (n2 - n_hi.astype(F32)).astype(BF16)
    w_hi = wr.astype(BF16)
    w_lo = (wr - w_hi.astype(F32)).astype(BF16)
    logits = (jnp.dot(n_hi, w_hi, preferred_element_type=F32)
              + jnp.dot(n_hi, w_lo, preferred_element_type=F32)
              + jnp.dot(n_lo, w_hi, preferred_element_type=F32))
    lane = lax.broadcasted_iota(I32, logits.shape, 1)
    logits = jnp.where(lane < n_experts, logits, -jnp.inf)
    mx = jnp.max(logits, axis=-1, keepdims=True)
    e = jnp.exp(logits - mx)
    aff = e / jnp.sum(e, axis=-1, keepdims=True)
    aff_ref[...] = aff
    afft_ref[0] = aff.T


def _mix_out(o_f, o_b, proj_gdn, attn, x2d, gdn_norm_w, subln_w, w_out_bf16, norm2_w, w_router,
             *, seq, tm, post_scale):
    m, d = x2d.shape
    dg = o_f.shape[1]
    n_gdn_heads = dg // HEAD
    n_diff_heads = attn.shape[1] // (2 * HEAD)
    n_experts = w_router.shape[1]
    nsb = seq // tm
    gw = jnp.tile(gdn_norm_w.reshape(1, HEAD), (1, 1))
    sw = subln_w.reshape(1, 2 * HEAD)
    wr = jnp.zeros((d, LANES), F32).at[:, :n_experts].set(w_router)
    zblk = proj_gdn.shape[1] // dg - 1
    row = lambda i: (i, 0)
    const = lambda i: (0, 0)
    return pl.pallas_call(
        functools.partial(_mix_out_kernel, n_gdn_heads=n_gdn_heads, n_diff_heads=n_diff_heads,
                          post_scale=post_scale, n_experts=n_experts),
        out_shape=(jax.ShapeDtypeStruct((m, d), F32),
                   jax.ShapeDtypeStruct((m, d // 2 // LANES, LANES), U32),
                   jax.ShapeDtypeStruct((m, LANES), F32),
                   jax.ShapeDtypeStruct((m // seq, LANES, seq), F32)),
        grid=(m // tm,),
        in_specs=[pl.BlockSpec((tm, dg), row), pl.BlockSpec((tm, dg), row),
                  pl.BlockSpec((tm, dg), lambda i: (i, zblk)),
                  pl.BlockSpec((tm, attn.shape[1]), row),
                  pl.BlockSpec((tm, d), row),
                  pl.BlockSpec((1, HEAD), const), pl.BlockSpec((1, 2 * HEAD), const),
                  pl.BlockSpec(w_out_bf16.shape, const),
                  pl.BlockSpec((1, d), const), pl.BlockSpec((d, LANES), const)],
        out_specs=(pl.BlockSpec((tm, d), row),
                   pl.BlockSpec((tm, d // 2 // LANES, LANES), lambda i: (i, 0, 0)),
                   pl.BlockSpec((tm, LANES), row),
                   pl.BlockSpec((1, LANES, tm), lambda i: (i // nsb, 0, i % nsb))),
        scratch_shapes=[pltpu.VMEM((tm, d), BF16)],
        compiler_params=_cparams(("parallel",)),
        name="mix_out",
    )(o_f, o_b, proj_gdn, attn, x2d, gw, sw, w_out_bf16, norm2_w.reshape(1, d), wr)


def _topk_kernel(afft_ref, idx_ref, gate_ref, post_ref, cs_ref, pos_scr, *, cap):
    x = afft_ref[0]
    ne, s = x.shape
    bits = pltpu.bitcast(x, I32)

    def bisect(i, prefix):
        cand = prefix | (jnp.int32(1) << (30 - i))
        cnt = jnp.sum((bits >= cand).astype(I32), axis=-1, keepdims=True)
        return jnp.where(cnt >= cap, cand, prefix)

    thr = lax.fori_loop(0, 31, bisect, jnp.zeros((ne, 1), I32))
    gt = bits > thr
    eq = bits == thr
    need = cap - jnp.sum(gt.astype(I32), axis=-1, keepdims=True)

    ri = lax.broadcasted_iota(I32, (LANES, LANES), 0)
    ci = lax.broadcasted_iota(I32, (LANES, LANES), 1)
    upper_strict = (ri < ci).astype(BF16)

    def excl_prefix(mask):
        carry = jnp.zeros((ne, 1), F32)
        parts, starts = [], []
        for blk in range(s // LANES):
            mb = mask[:, blk * LANES:(blk + 1) * LANES].astype(BF16)
            parts.append(jnp.dot(mb, upper_strict, preferred_element_type=F32) + carry)
            starts.append(carry)
            carry = carry + jnp.sum(mb.astype(F32), axis=-1, keepdims=True)
        return jnp.concatenate(parts, axis=1), jnp.concatenate(starts, axis=1)

    eq_rank, _ = excl_prefix(eq)
    sel = jnp.logical_or(gt, jnp.logical_and(eq, eq_rank < need.astype(F32)))
    pos, starts = excl_prefix(sel)

    posm = jnp.where(sel, pos, -1.0)
    pos_scr[...] = posm
    cs_ref[0] = starts.astype(I32)
    posm_pad = jnp.concatenate([posm, jnp.full((LANES - ne, s), -1.0, F32)], axis=0)
    for blk in range(s // LANES):
        post_ref[0, blk * LANES:(blk + 1) * LANES, :] = posm_pad[:, blk * LANES:(blk + 1) * LANES].T

    tb = min(s, 512)
    tok = lax.broadcasted_iota(I32, (1, tb), 1)
    slot = lax.broadcasted_iota(I32, (cap, tb), 0).astype(F32)
    zero = jnp.zeros((3, tb), F32)

    def per_expert(e, carry):
        r = jnp.zeros((8, cap), F32)
        for blk in range(s // tb):
            prow = pos_scr[pl.ds(e, 1), blk * tb:(blk + 1) * tb]
            onehot = (prow == slot).astype(BF16)
            a = afft_ref[0, pl.ds(e, 1), blk * tb:(blk + 1) * tb]
            a_hi = a.astype(BF16).astype(F32)
            a_mid = (a - a_hi).astype(BF16).astype(F32)
            a_lo = a - a_hi - a_mid
            t = tok + blk * tb
            lhs = jnp.concatenate([(t // 64).astype(F32), (t % 64).astype(F32),
                                   a_hi, a_mid, a_lo, zero], axis=0)
            r = r + _mm_nt(lhs, onehot)
        idx_ref[0, pl.ds(e, 1), :] = (r[0:1] * 64.0 + r[1:2]).astype(I32)
        gate_ref[0, pl.ds(e, 1), :] = r[2:3] + r[3:4] + r[4:5]
        return carry

    lax.fori_loop(0, ne, per_expert, 0)


def _topk(afft, n_experts, cap):
    b, _, s = afft.shape
    return pl.pallas_call(
        functools.partial(_topk_kernel, cap=cap),
        out_shape=(jax.ShapeDtypeStruct((b, n_experts, cap), I32),
                   jax.ShapeDtypeStruct((b, n_experts, cap), F32),
                   jax.ShapeDtypeStruct((b, s, LANES), F32),
                   jax.ShapeDtypeStruct((b, n_experts, s // LANES), I32)),
        grid=(b,),
        in_specs=[pl.BlockSpec((1, n_experts, s), lambda i: (i, 0, 0))],
        out_specs=(pl.BlockSpec((1, n_experts, cap), lambda i: (i, 0, 0)),
                   pl.BlockSpec((1, n_experts, cap), lambda i: (i, 0, 0)),
                   pl.BlockSpec((1, s, LANES), lambda i: (i, 0, 0)),
                   pl.BlockSpec((1, n_experts, s // LANES), lambda i: (i, 0, 0))),
        scratch_shapes=[pltpu.VMEM((n_experts, s), F32)],
        compiler_params=_cparams(("parallel",)),
        name="ec_topk",
    )(afft)


def _ffn_kernel(rows_ref, src_ref, wg_ref, wu_ref, wd_ref, gate_ref, o_ref,
                acc_scr, gbuf, x_scr, sem, *, tmr, n_f, n_blocks):
    f = pl.program_id(2)
    blk = pl.program_id(0) * pl.num_programs(1) + pl.program_id(1)
    step = blk * n_f + f
    per_step = tmr // n_f
    unroll = 8

    def row_copy(b, i):
        return pltpu.make_async_copy(src_ref.at[rows_ref[b * tmr + i]], gbuf.at[i], sem)

    def wait_block(b):
        def body(i, carry):
            for u in range(unroll):
                row_copy(b, i * unroll + u).wait()
            return carry
        lax.fori_loop(0, tmr // unroll, body, 0)

    @pl.when(step == 0)
    def _():
        acc_scr[...] = jnp.zeros(acc_scr.shape, F32)

        def body(i, carry):
            for u in range(unroll):
                row_copy(0, i * unroll + u).start()
            return carry
        lax.fori_loop(0, tmr // unroll, body, 0)

    @pl.when(f == 0)
    def _():
        wait_block(blk)
        hd = x_scr.shape[1] // 2
        tiles = pltpu.einshape("mkd->kmd", gbuf[...])
        for k in range(hd // LANES):
            words = tiles[k]
            x_scr[:, k * LANES:(k + 1) * LANES] = (
                pltpu.bitcast(words & jnp.uint32(0xFFFF0000), F32).astype(BF16))
            x_scr[:, hd + k * LANES:hd + (k + 1) * LANES] = (
                pltpu.bitcast(words << 16, F32).astype(BF16))

    nxt = jnp.minimum(blk + 1, n_blocks - 1)
    for i in range(per_step):
        row_copy(nxt, f * per_step + i).start()

    x = x_scr[...]
    g = jnp.dot(x, wg_ref[0, 0].astype(BF16), preferred_element_type=F32)
    u = jnp.dot(x, wu_ref[0, 0].astype(BF16), preferred_element_type=F32)
    hid = (g * _sigmoid(g) * u).astype(BF16)
    part = jnp.dot(hid, wd_ref[0, 0].astype(BF16), preferred_element_type=F32)
    acc = jnp.where(f == 0, part, acc_scr[...] + part)
    acc_scr[...] = acc
    o_ref[0] = acc * gate_ref[0]

    @pl.when(step == n_blocks * n_f - 1)
    def _():
        wait_block(nxt)


def _expert_ffn(rows_flat, packed, w_gate, w_up, w_down, gate_col, *, layer, rows, tmr, tf):
    ne, d, dff = w_gate.shape[1], w_gate.shape[2], w_gate.shape[3]
    n_r = rows // tmr
    n_f = dff // tf
    return pl.pallas_call(
        functools.partial(_ffn_kernel, tmr=tmr, n_f=n_f, n_blocks=ne * n_r),
        out_shape=jax.ShapeDtypeStruct((ne, rows, d), F32),
        grid_spec=pltpu.PrefetchScalarGridSpec(
            num_scalar_prefetch=1,
            grid=(ne, n_r, n_f),
            in_specs=[pl.BlockSpec(memory_space=pl.ANY),
                      pl.BlockSpec((1, 1, d, tf), lambda e, r, f, idx: (layer, e, 0, f)),
                      pl.BlockSpec((1, 1, d, tf), lambda e, r, f, idx: (layer, e, 0, f)),
                      pl.BlockSpec((1, 1, tf, d), lambda e, r, f, idx: (layer, e, f, 0)),
                      pl.BlockSpec((1, tmr, 1), lambda e, r, f, idx: (e, r, 0))],
            out_specs=pl.BlockSpec((1, tmr, d), lambda e, r, f, idx: (e, r, 0)),
            scratch_shapes=[pltpu.VMEM((tmr, d), F32),
                            pltpu.VMEM((tmr, d // 2 // LANES, LANES), U32),
                            pltpu.VMEM((tmr, d), BF16), pltpu.SemaphoreType.DMA(())],
        ),
        compiler_params=_cparams(("arbitrary", "arbitrary", "arbitrary")),
        name="ec_ffn",
    )(rows_flat, packed, w_gate, w_up, w_down, gate_col)


def _combine_kernel(cs_ref, nr_ref, pos_ref, x1_ref, fw_ref, y_hbm, o_ref, slab, sem, *,
                    w, cap, ne, nblk, final_norm):
    j = pl.program_id(1)
    step = pl.program_id(0) * nblk + j
    nsteps = pl.num_programs(0) * nblk
    cur = step % 2
    per = LANES // w

    def window(st, e, r):
        first = cs_ref[st * ne + e] + r * (w - SUBLANES)
        start = jnp.minimum((first // SUBLANES) * SUBLANES, cap - w)
        return first, pl.multiple_of(start, SUBLANES)

    def copy(st, e, r, buf):
        _, start = window(st, e, r)
        row = pl.multiple_of((st // nblk) * cap + start, SUBLANES)
        return pltpu.make_async_copy(y_hbm.at[e, pl.ds(row, w), :],
                                     slab.at[buf, pl.ds(e * w, w), :], sem.at[buf])

    def fetch(st, r, buf):
        for e in range(ne):
            copy(st, e, r, buf).start()

    def drain(st, r, buf):
        for e in range(ne):
            copy(st, e, r, buf).wait()

    @pl.when(step == 0)
    def _():
        fetch(0, 0, 0)

    drain(step, 0, cur)

    @pl.when(step + 1 < nsteps)
    def _():
        fetch(step + 1, 0, 1 - cur)

    pos = pos_ref[0]
    lane = lax.broadcasted_iota(I32, pos.shape, 1)
    lane_f = lane.astype(F32)

    def contrib(r, buf):
        cols = []
        for grp in range(ne // per):
            e = grp * per + per - 1
            first, start = window(step, e, r)
            pe = pos[:, e:e + 1]
            first_v = first.astype(F32)
            base_v = (start - (per - 1) * w).astype(F32)
            for q in range(per - 2, -1, -1):
                e = grp * per + q
                first, start = window(step, e, r)
                m = lane < (q + 1) * w
                pe = jnp.where(m, pos[:, e:e + 1], pe)
                first_v = jnp.where(m, first.astype(F32), first_v)
                base_v = jnp.where(m, (start - q * w).astype(F32), base_v)
            hit = jnp.logical_and(jnp.logical_and(pe >= first_v, pe < first_v + (w - SUBLANES)),
                                  pe - base_v == lane_f)
            cols.append(hit.astype(BF16))
        sel = jnp.concatenate(cols, axis=1)
        y = slab[buf]
        y_hi = y.astype(BF16)
        y_lo = (y - y_hi.astype(F32)).astype(BF16)
        return (jnp.dot(sel, y_hi, preferred_element_type=F32)
                + jnp.dot(sel, y_lo, preferred_element_type=F32))

    acc = x1_ref[...] + contrib(0, cur)

    def extra_round(r, acc):
        fetch(step, r, cur)
        drain(step, r, cur)
        return acc + contrib(r, cur)

    acc = lax.fori_loop(1, nr_ref[step], extra_round, acc)
    if final_norm:
        acc = acc * lax.rsqrt(jnp.mean(acc * acc, axis=-1, keepdims=True) + NORM_EPS) * fw_ref[...]
    o_ref[...] = acc


def _combine(cs_flat, nr_flat, post, x1, y, final_w, *, seq, tb, w, cap, final_norm):
    m, d = x1.shape
    ne = y.shape[0]
    nblk = seq // tb
    return pl.pallas_call(
        functools.partial(_combine_kernel, w=w, cap=cap, ne=ne, nblk=nblk, final_norm=final_norm),
        out_shape=jax.ShapeDtypeStruct((m, d), F32),
        grid_spec=pltpu.PrefetchScalarGridSpec(
            num_scalar_prefetch=2,
            grid=(m // seq, nblk),
            in_specs=[pl.BlockSpec((1, tb, LANES), lambda b, j, cs, nr: (b, j, 0)),
                      pl.BlockSpec((tb, d), lambda b, j, cs, nr: (b * nblk + j, 0)),
                      pl.BlockSpec((1, d), lambda b, j, cs, nr: (0, 0)),
                      pl.BlockSpec(memory_space=pl.ANY)],
            out_specs=pl.BlockSpec((tb, d), lambda b, j, cs, nr: (b * nblk + j, 0)),
            scratch_shapes=[pltpu.VMEM((2, ne * w, d), F32), pltpu.SemaphoreType.DMA((2,))],
        ),
        compiler_params=_cparams(("arbitrary", "arbitrary")),
        name="ec_combine",
    )(cs_flat, nr_flat, post, x1, final_w.reshape(1, d), y)


def _pick(n, pref):
    t = min(n, pref)
    while n % t:
        t //= 2
    return t


def kernel(x, norm1_w, w_in, conv_w, a_log, dt_bias, gdn_norm_w, diff_lambda, diff_subln_w, w_out,
           norm2_w, w_router, w_gate, w_up, w_down, final_norm_w):
    bsz, seq, d = x.shape
    depth = w_in.shape[0]
    d_gdn = d // 2
    d_diff = d - d_gdn
    n_gdn_heads = d_gdn // HEAD
    n_diff_heads = d_diff // (2 * HEAD)
    n_gate = N_DIR * n_gdn_heads
    n_experts = w_router.shape[2]
    cap = EC_CAPACITY * seq // n_experts
    m = bsz * seq
    c_gate = 4 * d_gdn
    c_diff = c_gate + 2 * n_gate

    half = HEAD // 2
    inv_freq = ROPE_THETA ** (-jnp.arange(half, dtype=F32) / half)
    ang = jnp.arange(seq, dtype=F32)[:, None] * inv_freq[None, :]
    cos_t = jnp.concatenate([jnp.cos(ang), jnp.cos(ang)], axis=1)
    sin_t = jnp.concatenate([-jnp.sin(ang), jnp.sin(ang)], axis=1)

    tm_proj = _pick(seq, 1024)
    x2 = x.reshape(m, d)
    for l in range(depth):
        w_l = w_in[l]
        w_gdn = w_l[:, :c_gate].astype(BF16)
        w_gt = jnp.zeros((d, LANES), BF16).at[:, :2 * n_gate].set(w_l[:, c_gate:c_diff].astype(BF16))
        w_df = w_l[:, c_diff:].astype(BF16)

        proj_gdn = _norm_proj(x2, norm1_w[l], w_gdn, tm=tm_proj, tn=PROJ_TN, out_dtype=F32)
        gates = _norm_proj(x2, norm1_w[l], w_gt, tm=tm_proj, tn=LANES, out_dtype=F32)
        n_blk = d_diff // PROJ_TN
        proj_diff = _norm_proj(x2, norm1_w[l], w_df, tm=tm_proj, tn=PROJ_TN, out_dtype=BF16,
                               rot=(cos_t, sin_t, seq, n_blk, 2 * n_blk))

        qkv = _gdn_prep(proj_gdn.reshape(bsz, seq, 4 * d_gdn), conv_w[l], n_gdn_heads)
        gcol, grow = _gate_prep(gates, a_log[l], dt_bias[l], seq=seq, tm=_pick(seq, 512))
        o_f, o_b = _gdn_scan(qkv, gcol.reshape(bsz, seq, LANES), grow, n_gdn_heads, rows=_pick(seq, SCAN_ROWS))

        lf = diff_lambda[l].astype(F32)
        lambda_init = 0.8 - 0.6 * math.exp(-0.3 * l)
        lam = (jnp.exp(jnp.sum(lf[0] * lf[1])) - jnp.exp(jnp.sum(lf[2] * lf[3])) + lambda_init)
        attn = _diff_attn(proj_diff.reshape(bsz, seq, 3 * d_diff), lam.reshape(1).astype(F32),
                          n_diff_heads, tq=_pick(seq, ATTN_TQ))

        x1, n2, _, afft = _mix_out(
            o_f.reshape(m, d_gdn), o_b.reshape(m, d_gdn), proj_gdn, attn.reshape(m, d_diff), x2,
            gdn_norm_w[l], diff_subln_w[l], w_out[l].astype(BF16), norm2_w[l], w_router[l],
            seq=seq, tm=_pick(seq, 256), post_scale=1.0 - lambda_init)

        idx, gate, post, cs128 = _topk(afft, n_experts, cap)
        rows = (idx + (jnp.arange(bsz, dtype=I32) * seq)[:, None, None]).transpose(1, 0, 2)
        gate_col = gate.transpose(1, 0, 2).reshape(n_experts, bsz * cap, 1)
        y = _expert_ffn(rows.reshape(-1), n2, w_gate, w_up, w_down, gate_col, layer=l,
                        rows=bsz * cap, tmr=_pick(bsz * cap, FFN_TM), tf=FFN_TF)
        tb = _pick(seq, COMBINE_TB)
        win = min(COMBINE_W, cap)
        cs = cs128[:, :, ::tb // LANES].transpose(0, 2, 1)
        run = jnp.concatenate([cs[:, 1:], jnp.full((bsz, 1, n_experts), cap, I32)], axis=1) - cs
        per_round = win - SUBLANES
        rounds = jnp.maximum(1, jnp.max((run + per_round - 1) // per_round, axis=-1))
        x2 = _combine(cs.reshape(-1), rounds.reshape(-1).astype(I32), post, x1, y, final_norm_w,
                      seq=seq, tb=tb, w=win, cap=cap, final_norm=l == depth - 1)
    return x2.reshape(bsz, seq, d)
```

```python
import functools
import math

import jax
import jax.numpy as jnp
from jax import lax
from jax.experimental import pallas as pl
from jax.experimental.pallas import tpu as pltpu

F32 = jnp.float32
BF16 = jnp.bfloat16
I32 = jnp.int32
U32 = jnp.uint32

NORM_EPS = 1e-6
LANES = 128
SUBLANES = 8
HEAD = 128
GDN_CONV = 5
CHUNK = 64
N_DIR = 2
N_EXPERTS = 16
EC_CAPACITY = 2
ROPE_THETA = 10000.0
VMEM_LIMIT = 56 * 1024 * 1024
HI = lax.Precision.HIGHEST
SCAN_ROWS = 256
PREP_ROWS = 512
ATTN_TQ = 256
ATTN_KC = 512
PROJ_TN = 1024
FFN_TM = 1024
FFN_TF = 256
COMBINE_TB = 256
COMBINE_W = 64


def _cparams(sem):
    return pltpu.CompilerParams(dimension_semantics=sem, vmem_limit_bytes=VMEM_LIMIT)


def _sigmoid(x):
    return 1.0 / (1.0 + jnp.exp(-x))


def _mm(a, b):
    return jnp.dot(a.astype(BF16), b.astype(BF16), preferred_element_type=F32)


def _mm_nt(a, b):
    return lax.dot_general(a.astype(BF16), b.astype(BF16), (((1,), (1,)), ((), ())),
                           preferred_element_type=F32)


def _mm_tn(a, b):
    return lax.dot_general(a.astype(BF16), b.astype(BF16), (((0,), (0,)), ((), ())),
                           preferred_element_type=F32)


def _norm_rows(x_ref, nw_ref, n_scr):
    x = x_ref[...]
    ms = jnp.mean(x * x, axis=-1, keepdims=True)
    n_scr[...] = (x * lax.rsqrt(ms + NORM_EPS) * nw_ref[...]).astype(BF16)


def _norm_proj_kernel(x_ref, nw_ref, w_ref, ws_ref, o_ref, os_ref, n_scr):
    @pl.when(pl.program_id(1) == 0)
    def _():
        _norm_rows(x_ref, nw_ref, n_scr)
        os_ref[...] = jnp.dot(n_scr[...], ws_ref[...], preferred_element_type=F32)

    o_ref[...] = jnp.dot(n_scr[...], w_ref[...], preferred_element_type=F32).astype(o_ref.dtype)


def _norm_proj_rot_kernel(x_ref, nw_ref, w_ref, cos_ref, sin_ref, o_ref, n_scr, *,
                          n_q_blocks, n_rot_blocks):
    j = pl.program_id(1)

    @pl.when(j == 0)
    def _():
        _norm_rows(x_ref, nw_ref, n_scr)

    acc = jnp.dot(n_scr[...], w_ref[...], preferred_element_type=F32)
    tn = acc.shape[1]
    is_rot = j < n_rot_blocks
    scale = jnp.where(j < n_q_blocks, HEAD ** -0.5, 1.0).astype(F32)
    c = cos_ref[...]
    s = sin_ref[...]
    for g in range(tn // HEAD):
        t = acc[:, g * HEAD:(g + 1) * HEAD]
        r = t * c + pltpu.roll(t, HEAD // 2, axis=1) * s
        o_ref[:, g * HEAD:(g + 1) * HEAD] = (jnp.where(is_rot, r, t) * scale).astype(o_ref.dtype)


def _norm_proj(x2d, nw, w_bf16, w_side, *, tm, tn):
    m, d = x2d.shape
    n = w_bf16.shape[1]
    ns = w_side.shape[1]
    return pl.pallas_call(
        _norm_proj_kernel,
        out_shape=(jax.ShapeDtypeStruct((m, n), F32), jax.ShapeDtypeStruct((m, ns), F32)),
        grid=(m // tm, n // tn),
        in_specs=[pl.BlockSpec((tm, d), lambda i, j: (i, 0)),
                  pl.BlockSpec((1, d), lambda i, j: (0, 0)),
                  pl.BlockSpec((d, tn), lambda i, j: (0, j)),
                  pl.BlockSpec((d, ns), lambda i, j: (0, 0))],
        out_specs=(pl.BlockSpec((tm, tn), lambda i, j: (i, j)),
                   pl.BlockSpec((tm, ns), lambda i, j: (i, 0))),
        scratch_shapes=[pltpu.VMEM((tm, d), BF16)],
        compiler_params=_cparams(("parallel", "arbitrary")),
        name="norm_proj",
    )(x2d, nw.reshape(1, d), w_bf16, w_side)


def _norm_proj_rot(x2d, nw, w_bf16, cos_t, sin_t, *, seq, tm, tn, n_q_blocks, n_rot_blocks):
    m, d = x2d.shape
    n = w_bf16.shape[1]
    nsb = seq // tm
    return pl.pallas_call(
        functools.partial(_norm_proj_rot_kernel, n_q_blocks=n_q_blocks, n_rot_blocks=n_rot_blocks),
        out_shape=jax.ShapeDtypeStruct((m, n), BF16),
        grid=(m // tm, n // tn),
        in_specs=[pl.BlockSpec((tm, d), lambda i, j: (i, 0)),
                  pl.BlockSpec((1, d), lambda i, j: (0, 0)),
                  pl.BlockSpec((d, tn), lambda i, j: (0, j)),
                  pl.BlockSpec((tm, HEAD), lambda i, j: (i % nsb, 0)),
                  pl.BlockSpec((tm, HEAD), lambda i, j: (i % nsb, 0))],
        out_specs=pl.BlockSpec((tm, tn), lambda i, j: (i, j)),
        scratch_shapes=[pltpu.VMEM((tm, d), BF16)],
        compiler_params=_cparams(("parallel", "arbitrary")),
        name="norm_proj_rot",
    )(x2d, nw.reshape(1, d), w_bf16, cos_t, sin_t)


def _gdn_prep_kernel(x_ref, w_ref, o_ref, *, n_heads):
    j = pl.program_id(1)
    w = w_ref[...]
    s = x_ref.shape[1]
    rows = min(s, PREP_ROWS)
    half = GDN_CONV // 2
    row = lax.broadcasted_iota(I32, (rows, HEAD), 0)
    is_q = j < n_heads
    is_qk = j < 2 * n_heads
    for r0 in range(0, s, rows):
        acc = x_ref[0, r0:r0 + rows, :] * w[half:half + 1, :]
        for k in range(GDN_CONV):
            off = k - half
            if off == 0:
                continue
            if 0 <= r0 + off and r0 + off + rows <= s:
                xs = x_ref[0, r0 + off:r0 + off + rows, :]
            else:
                xs = pltpu.roll(x_ref[0, r0:r0 + rows, :], (-off) % rows, axis=0)
                valid = jnp.logical_and(row + (r0 + off) >= 0, row + (r0 + off) < s)
                xs = jnp.where(valid, xs, 0.0)
            acc = acc + xs * w[k:k + 1, :]
        y = acc * _sigmoid(acc)
        ss = jnp.sum(y * y, axis=-1, keepdims=True)
        fac = jnp.where(is_qk, lax.rsqrt(ss + NORM_EPS) * jnp.where(is_q, HEAD ** -0.5, 1.0), 1.0)
        o_ref[0, r0:r0 + rows, :] = y * fac


def _gdn_prep(proj3d, conv_w, n_heads):
    b, s, _ = proj3d.shape
    ncol = 3 * n_heads
    return pl.pallas_call(
        functools.partial(_gdn_prep_kernel, n_heads=n_heads),
        out_shape=jax.ShapeDtypeStruct((b, s, ncol * HEAD), F32),
        grid=(b, ncol),
        in_specs=[pl.BlockSpec((1, s, HEAD), lambda i, j: (i, 0, j)),
                  pl.BlockSpec((GDN_CONV, HEAD), lambda i, j: (0, j))],
        out_specs=pl.BlockSpec((1, s, HEAD), lambda i, j: (i, 0, j)),
        compiler_params=_cparams(("parallel", "parallel")),
        name="gdn_prep",
    )(proj3d, conv_w)


def _gate_prep_kernel(x_ref, alog_ref, dtb_ref, col_ref, row_ref, *, n_gate):
    x = x_ref[...]
    lane = lax.broadcasted_iota(I32, x.shape, 1)
    beta = _sigmoid(x)
    a = x + dtb_ref[...]
    sp = jnp.maximum(a, 0.0) + jnp.log(1.0 + jnp.exp(-jnp.abs(a)))
    g = -jnp.exp(alog_ref[...]) * sp
    y = jnp.where(lane < n_gate, beta, jnp.where(lane < 2 * n_gate, g, 0.0))
    col_ref[...] = y
    row_ref[0] = y.T


def _gate_prep(gates2d, a_log, dt_bias, *, seq, tm):
    m = gates2d.shape[0]
    n_gate = a_log.size
    nsb = seq // tm
    pad = lambda v: jnp.zeros((1, LANES), F32).at[0, n_gate:2 * n_gate].set(v.reshape(-1).astype(F32))
    return pl.pallas_call(
        functools.partial(_gate_prep_kernel, n_gate=n_gate),
        out_shape=(jax.ShapeDtypeStruct((m, LANES), F32),
                   jax.ShapeDtypeStruct((m // seq, LANES, seq), F32)),
        grid=(m // tm,),
        in_specs=[pl.BlockSpec((tm, LANES), lambda i: (i, 0)),
                  pl.BlockSpec((1, LANES), lambda i: (0, 0)),
                  pl.BlockSpec((1, LANES), lambda i: (0, 0))],
        out_specs=(pl.BlockSpec((tm, LANES), lambda i: (i, 0)),
                   pl.BlockSpec((1, LANES, tm), lambda i: (i // nsb, 0, i % nsb))),
        compiler_params=_cparams(("parallel",)),
        name="gate_prep",
    )(gates2d, pad(a_log), pad(dt_bias))


def _gdn_scan_kernel(qkv_f_ref, qkv_b_ref, gcol_f_ref, gcol_b_ref, grow_f_ref, grow_b_ref,
                     of_ref, ob_ref, st_ref, *, n_heads, n_sub):
    @pl.when(pl.program_id(1) == 0)
    def _():
        st_ref[...] = jnp.zeros(st_ref.shape, F32)

    c = CHUNK
    ri = lax.broadcasted_iota(I32, (c, c), 0)
    ci = lax.broadcasted_iota(I32, (c, c), 1)
    eye = (ri == ci).astype(F32)
    strict = {False: ri > ci, True: ri < ci}
    incl = {False: ri >= ci, True: ri <= ci}
    lower_incl = (ri >= ci).astype(F32)
    upper_incl = (ri <= ci).astype(F32)
    ng = N_DIR * n_heads

    items = []
    for d in range(N_DIR):
        reverse = d == 1
        qkv_ref = qkv_b_ref if reverse else qkv_f_ref
        gcol_ref = gcol_b_ref if reverse else gcol_f_ref
        grow_ref = grow_b_ref if reverse else grow_f_ref
        for sub in range(n_sub):
            r0 = sub * c
            gcol = gcol_ref[0, r0:r0 + c, :]
            grow = grow_ref[0, ng:2 * ng, r0:r0 + c]
            gc_col = jnp.dot(upper_incl if reverse else lower_incl, gcol,
                             precision=HI, preferred_element_type=F32)
            gc_row = jnp.dot(grow, lower_incl if reverse else upper_incl,
                             precision=HI, preferred_element_type=F32)
            last = 0 if reverse else c - 1
            for h in range(n_heads):
                gi = d * n_heads + h
                it = dict(d=d, sub=sub, h=h, gi=gi, r0=r0, reverse=reverse)
                it["q"] = qkv_ref[0, r0:r0 + c, h * HEAD:(h + 1) * HEAD]
                it["k"] = qkv_ref[0, r0:r0 + c, (n_heads + h) * HEAD:(n_heads + h + 1) * HEAD]
                it["v"] = qkv_ref[0, r0:r0 + c, (2 * n_heads + h) * HEAD:(2 * n_heads + h + 1) * HEAD]
                it["beta"] = gcol[:, gi:gi + 1]
                it["gc_c"] = gc_col[:, ng + gi:ng + gi + 1]
                it["gc_r"] = gc_row[gi:gi + 1, :]
                it["gtot"] = it["gc_r"][:, last:last + 1]
                items.append(it)

    for it in items:
        m_incl = incl[it["reverse"]]
        gdiff = it["gc_c"] - it["gc_r"]
        it["decay"] = jnp.where(m_incl, jnp.exp(jnp.where(m_incl, gdiff, 0.0)), 0.0)
        it["kb"] = it["k"] * it["beta"]
    for it in items:
        it["aq"] = _mm_nt(jnp.concatenate([it["kb"], it["q"]], axis=0), it["k"])
    for it in items:
        low = jnp.where(strict[it["reverse"]], it["aq"][:c] * it["decay"], 0.0)
        it["qk"] = it["aq"][c:] * it["decay"]
        it["p"] = low
        it["inv"] = eye - low
    for _ in range(int(math.log2(c)) - 1):
        for it in items:
            it["p"] = _mm(it["p"], it["p"])
        for it in items:
            it["inv"] = it["inv"] + _mm(it["inv"], it["p"])
    for it in items:
        eg = jnp.exp(it["gc_c"])
        rhs = jnp.concatenate([it["v"] * it["beta"], it["kb"] * eg], axis=1)
        it["qg"] = it["q"] * eg
        it["kd"] = it["k"] * jnp.exp(it["gtot"] - it["gc_c"])
        it["sol"] = _mm(it["inv"], rhs)

    for step in range(n_sub):
        cur = [it for it in items if it["sub"] == (n_sub - 1 - step if it["reverse"] else step)]
        for it in cur:
            it["state"] = st_ref[it["gi"]]
            w = it["sol"][:, HEAD:]
            it["ws"] = _mm(jnp.concatenate([w, it["qg"]], axis=0), it["state"])
        for it in cur:
            it["v_new"] = it["sol"][:, :HEAD] - it["ws"][:c]
        for it in cur:
            o = it["ws"][c:] + _mm(it["qk"], it["v_new"])
            o_ref = ob_ref if it["reverse"] else of_ref
            o_ref[0, it["r0"]:it["r0"] + c, it["h"] * HEAD:(it["h"] + 1) * HEAD] = o
        for it in cur:
            st_ref[it["gi"]] = it["state"] * jnp.exp(it["gtot"]) + _mm_tn(it["kd"], it["v_new"])


def _gdn_scan(qkv, gcol, grow, n_heads, *, rows):
    b, s, _ = qkv.shape
    nblk = s // rows
    n_sub = rows // CHUNK
    dg = n_heads * HEAD
    fwd = lambda i, t: (i, t, 0)
    bwd = lambda i, t: (i, nblk - 1 - t, 0)
    out = jax.ShapeDtypeStruct((b, s, dg), F32)
    return pl.pallas_call(
        functools.partial(_gdn_scan_kernel, n_heads=n_heads, n_sub=n_sub),
        out_shape=(out, out),
        grid=(b, nblk),
        in_specs=[pl.BlockSpec((1, rows, 3 * dg), fwd),
                  pl.BlockSpec((1, rows, 3 * dg), bwd),
                  pl.BlockSpec((1, rows, LANES), fwd),
                  pl.BlockSpec((1, rows, LANES), bwd),
                  pl.BlockSpec((1, LANES, rows), lambda i, t: (i, 0, t)),
                  pl.BlockSpec((1, LANES, rows), lambda i, t: (i, 0, nblk - 1 - t))],
        out_specs=(pl.BlockSpec((1, rows, dg), fwd), pl.BlockSpec((1, rows, dg), bwd)),
        scratch_shapes=[pltpu.VMEM((N_DIR * n_heads, HEAD, HEAD), F32)],
        compiler_params=_cparams(("parallel", "arbitrary")),
        name="gdn_scan",
    )(qkv, qkv, gcol, gcol, grow, grow)


def _diff_attn_kernel(lam_ref, q_ref, k_ref, v_ref, o_ref, s_scr, m_scr, *, kc):
    @pl.when(pl.program_id(0) == 0)
    def _():
        s_scr[...] = jnp.zeros(s_scr.shape, F32)
        m_scr[...] = jnp.zeros(m_scr.shape, F32)

    for parity in range(2):
        @pl.when(pl.program_id(0) % 2 == parity)
        def _():
            _diff_attn_step(lam_ref, q_ref, k_ref, v_ref, o_ref, s_scr, m_scr,
                            cur=parity, prev=1 - parity, kc=kc)


def _diff_attn_step(lam_ref, q_ref, k_ref, v_ref, o_ref, s_scr, m_scr, *, cur, prev, kc):
    lam = lam_ref[0]
    q = q_ref[0]
    seq = k_ref.shape[1]
    nchunk = seq // kc
    halves = []
    for t in range(2):
        qt = q[:, t * HEAD:(t + 1) * HEAD]
        m_prev = m_scr[prev, t]
        m_lane = None
        l_lane = None
        acc = None
        for c in range(nchunk):
            kt = k_ref[0, c * kc:(c + 1) * kc, t * HEAD:(t + 1) * HEAD]
            s = lax.dot_general(qt, kt, (((1,), (1,)), ((), ())), preferred_element_type=F32)
            s_scr[cur, t, :, c * kc:(c + 1) * kc] = s
            for g in range(kc // LANES):
                sg = s[:, g * LANES:(g + 1) * LANES]
                m_lane = sg if m_lane is None else jnp.maximum(m_lane, sg)
            sp = s_scr[prev, t, :, c * kc:(c + 1) * kc]
            e = jnp.exp(sp - jnp.concatenate([m_prev] * (kc // LANES), axis=1))
            for g in range(kc // LANES):
                eg = e[:, g * LANES:(g + 1) * LANES]
                l_lane = eg if l_lane is None else l_lane + eg
            pv = jnp.dot(e.astype(BF16), v_ref[0, c * kc:(c + 1) * kc, :],
                         preferred_element_type=F32)
            acc = pv if acc is None else acc + pv
        m_scr[cur, t] = jnp.broadcast_to(jnp.max(m_lane, axis=-1, keepdims=True), m_lane.shape)
        halves.append(acc * (1.0 / jnp.sum(l_lane, axis=-1, keepdims=True)))
    o_ref[0] = halves[0] - lam * halves[1]


def _diff_attn(qkv3d, lam, n_heads, *, tq):
    b, s, _ = qkv3d.shape
    dv = 2 * HEAD
    nq = s // tq
    total = b * n_heads * nq

    def split(step):
        step = jnp.clip(step, 0, total - 1)
        return step // (n_heads * nq), (step // nq) % n_heads, step % nq

    def q_map(i):
        bi, h, t = split(i)
        return bi, t, h

    def k_map(i):
        bi, h, _ = split(i)
        return bi, 0, n_heads + h

    def v_map(i):
        bi, h, _ = split(i - 1)
        return bi, 0, 2 * n_heads + h

    def o_map(i):
        bi, h, t = split(i - 1)
        return bi, t, h

    return pl.pallas_call(
        functools.partial(_diff_attn_kernel, kc=_pick(s, ATTN_KC)),
        out_shape=jax.ShapeDtypeStruct((b, s, n_heads * dv), F32),
        grid_spec=pltpu.PrefetchScalarGridSpec(
            num_scalar_prefetch=0,
            grid=(total + 1,),
            in_specs=[pl.BlockSpec(memory_space=pltpu.SMEM),
                      pl.BlockSpec((1, tq, dv), q_map),
                      pl.BlockSpec((1, s, dv), k_map),
                      pl.BlockSpec((1, s, dv), v_map)],
            out_specs=pl.BlockSpec((1, tq, dv), o_map),
            scratch_shapes=[pltpu.VMEM((2, 2, tq, s), F32), pltpu.VMEM((2, 2, tq, LANES), F32)],
        ),
        compiler_params=_cparams(("arbitrary",)),
        name="diff_attn",
    )(lam, qkv3d, qkv3d, qkv3d)


def _mix_out_kernel(of_ref, ob_ref, z_ref, at_ref, x_ref, gw_ref, sw_ref, wo_ref, n2w_ref, wr_ref,
                    x1_ref, n2_ref, afft_ref, mix_scr, *, n_gdn_heads, n_diff_heads,
                    post_scale, n_experts):
    o = of_ref[...] + ob_ref[...]
    z = z_ref[...]
    gw = gw_ref[...]
    for h in range(n_gdn_heads):
        sl = slice(h * HEAD, (h + 1) * HEAD)
        seg = o[:, sl]
        y = seg * lax.rsqrt(jnp.mean(seg * seg, axis=-1, keepdims=True) + NORM_EPS) * gw
        zz = z[:, sl]
        mix_scr[:, sl] = (y * (zz * _sigmoid(zz))).astype(BF16)
    base = n_gdn_heads * HEAD
    at = at_ref[...]
    sw = sw_ref[...]
    dv = 2 * HEAD
    for h in range(n_diff_heads):
        seg = at[:, h * dv:(h + 1) * dv]
        y = seg * lax.rsqrt(jnp.mean(seg * seg, axis=-1, keepdims=True) + NORM_EPS) * sw
        mix_scr[:, base + h * dv:base + (h + 1) * dv] = (y * post_scale).astype(BF16)
    x1 = x_ref[...] + jnp.dot(mix_scr[...], wo_ref[...], preferred_element_type=F32)
    x1_ref[...] = x1
    n2 = x1 * lax.rsqrt(jnp.mean(x1 * x1, axis=-1, keepdims=True) + NORM_EPS) * n2w_ref[...]
    n_hi = n2.astype(BF16)
    hd = n2.shape[1] // 2
    bits = pltpu.bitcast(n_hi.astype(F32), U32)
    words = bits[:, :hd] | (bits[:, hd:] >> 16)
    chunks = jnp.stack([words[:, k * LANES:(k + 1) * LANES] for k in range(hd // LANES)], axis=0)
    n2_ref[...] = pltpu.einshape("kmd->mkd", chunks)
    wr = wr_ref[...]
    n_lo = (n2 - n_hi.astype(F32)).astype(BF16)
    w_hi = wr.astype(BF16)
    w_lo = (wr - w_hi.astype(F32)).astype(BF16)
    logits = (jnp.dot(n_hi, w_hi, preferred_element_type=F32)
              + jnp.dot(n_hi, w_lo, preferred_element_type=F32)
              + jnp.dot(n_lo, w_hi, preferred_element_type=F32))
    lane = lax.broadcasted_iota(I32, logits.shape, 1)
    logits = jnp.where(lane < n_experts, logits, -jnp.inf)
    mx = jnp.max(logits, axis=-1, keepdims=True)
    e = jnp.exp(logits - mx)
    aff = e / jnp.sum(e, axis=-1, keepdims=True)
    afft_ref[0] = aff.T


def _mix_out(o_f, o_b, proj_gdn, attn, x2d, gdn_norm_w, subln_w, w_out_bf16, norm2_w, w_router,
             *, seq, tm, post_scale):
    m, d = x2d.shape
    dg = o_f.shape[1]
    n_gdn_heads = dg // HEAD
    n_diff_heads = attn.shape[1] // (2 * HEAD)
    n_experts = w_router.shape[1]
    nsb = seq // tm
    gw = jnp.tile(gdn_norm_w.reshape(1, HEAD), (1, 1))
    sw = subln_w.reshape(1, 2 * HEAD)
    wr = jnp.zeros((d, LANES), F32).at[:, :n_experts].set(w_router)
    zblk = proj_gdn.shape[1] // dg - 1
    row = lambda i: (i, 0)
    const = lambda i: (0, 0)
    return pl.pallas_call(
        functools.partial(_mix_out_kernel, n_gdn_heads=n_gdn_heads, n_diff_heads=n_diff_heads,
                          post_scale=post_scale, n_experts=n_experts),
        out_shape=(jax.ShapeDtypeStruct((m, d), F32),
                   jax.ShapeDtypeStruct((m, d // 2 // LANES, LANES), U32),
                   jax.ShapeDtypeStruct((m // seq, LANES, seq), F32)),
        grid=(m // tm,),
        in_specs=[pl.BlockSpec((tm, dg), row), pl.BlockSpec((tm, dg), row),
                  pl.BlockSpec((tm, dg), lambda i: (i, zblk)),
                  pl.BlockSpec((tm, attn.shape[1]), row),
                  pl.BlockSpec((tm, d), row),
                  pl.BlockSpec((1, HEAD), const), pl.BlockSpec((1, 2 * HEAD), const),
                  pl.BlockSpec(w_out_bf16.shape, const),
                  pl.BlockSpec((1, d), const), pl.BlockSpec((d, LANES), const)],
        out_specs=(pl.BlockSpec((tm, d), row),
                   pl.BlockSpec((tm, d // 2 // LANES, LANES), lambda i: (i, 0, 0)),
                   pl.BlockSpec((1, LANES, tm), lambda i: (i // nsb, 0, i % nsb))),
        scratch_shapes=[pltpu.VMEM((tm, d), BF16)],
        compiler_params=_cparams(("parallel",)),
        name="mix_out",
    )(o_f, o_b, proj_gdn, attn, x2d, gw, sw, w_out_bf16, norm2_w.reshape(1, d), wr)


def _topk_kernel(afft_ref, idx_ref, gate_ref, post_ref, cs_ref, pos_scr, *, cap):
    x = afft_ref[0]
    ne, s = x.shape
    bits = pltpu.bitcast(x, I32)

    def bisect(i, prefix):
        cand = prefix | (jnp.int32(1) << (30 - i))
        cnt = jnp.sum((bits >= cand).astype(I32), axis=-1, keepdims=True)
        return jnp.where(cnt >= cap, cand, prefix)

    thr = lax.fori_loop(0, 31, bisect, jnp.zeros((ne, 1), I32))
    gt = bits > thr
    eq = bits == thr
    need = cap - jnp.sum(gt.astype(I32), axis=-1, keepdims=True)

    ri = lax.broadcasted_iota(I32, (LANES, LANES), 0)
    ci = lax.broadcasted_iota(I32, (LANES, LANES), 1)
    upper_strict = (ri < ci).astype(BF16)

    def excl_prefix(mask):
        carry = jnp.zeros((ne, 1), F32)
        parts, starts = [], []
        for blk in range(s // LANES):
            mb = mask[:, blk * LANES:(blk + 1) * LANES].astype(BF16)
            parts.append(jnp.dot(mb, upper_strict, preferred_element_type=F32) + carry)
            starts.append(carry)
            carry = carry + jnp.sum(mb.astype(F32), axis=-1, keepdims=True)
        return jnp.concatenate(parts, axis=1), jnp.concatenate(starts, axis=1)

    eq_rank, _ = excl_prefix(eq)
    sel = jnp.logical_or(gt, jnp.logical_and(eq, eq_rank < need.astype(F32)))
    pos, starts = excl_prefix(sel)

    posm = jnp.where(sel, pos, -1.0)
    pos_scr[...] = posm
    cs_ref[0] = starts.astype(I32)
    posm_pad = jnp.concatenate([posm, jnp.full((LANES - ne, s), -1.0, F32)], axis=0)
    for blk in range(s // LANES):
        post_ref[0, blk * LANES:(blk + 1) * LANES, :] = posm_pad[:, blk * LANES:(blk + 1) * LANES].T

    tb = min(s, 512)
    tok = lax.broadcasted_iota(I32, (1, tb), 1)
    slot = lax.broadcasted_iota(I32, (cap, tb), 0).astype(F32)
    zero = jnp.zeros((3, tb), F32)

    def per_expert(e, carry):
        r = jnp.zeros((8, cap), F32)
        for blk in range(s // tb):
            prow = pos_scr[pl.ds(e, 1), blk * tb:(blk + 1) * tb]
            onehot = (prow == slot).astype(BF16)
            a = afft_ref[0, pl.ds(e, 1), blk * tb:(blk + 1) * tb]
            a_hi = a.astype(BF16).astype(F32)
            a_mid = (a - a_hi).astype(BF16).astype(F32)
            a_lo = a - a_hi - a_mid
            t = tok + blk * tb
            lhs = jnp.concatenate([(t // 64).astype(F32), (t % 64).astype(F32),
                                   a_hi, a_mid, a_lo, zero], axis=0)
            r = r + _mm_nt(lhs, onehot)
        idx_ref[0, pl.ds(e, 1), :] = (r[0:1] * 64.0 + r[1:2]).astype(I32)
        gate_ref[0, pl.ds(e, 1), :] = r[2:3] + r[3:4] + r[4:5]
        return carry

    lax.fori_loop(0, ne, per_expert, 0)


def _topk(afft, n_experts, cap):
    b, _, s = afft.shape
    return pl.pallas_call(
        functools.partial(_topk_kernel, cap=cap),
        out_shape=(jax.ShapeDtypeStruct((b, n_experts, cap), I32),
                   jax.ShapeDtypeStruct((b, n_experts, cap), F32),
                   jax.ShapeDtypeStruct((b, s, LANES), F32),
                   jax.ShapeDtypeStruct((b, n_experts, s // LANES), I32)),
        grid=(b,),
        in_specs=[pl.BlockSpec((1, n_experts, s), lambda i: (i, 0, 0))],
        out_specs=(pl.BlockSpec((1, n_experts, cap), lambda i: (i, 0, 0)),
                   pl.BlockSpec((1, n_experts, cap), lambda i: (i, 0, 0)),
                   pl.BlockSpec((1, s, LANES), lambda i: (i, 0, 0)),
                   pl.BlockSpec((1, n_experts, s // LANES), lambda i: (i, 0, 0))),
        scratch_shapes=[pltpu.VMEM((n_experts, s), F32)],
        compiler_params=_cparams(("parallel",)),
        name="ec_topk",
    )(afft)


def _ffn_kernel(rows_ref, src_ref, wg_ref, wu_ref, wd_ref, gate_ref, o_ref,
                acc_scr, gbuf, x_scr, sem, *, tmr, n_f, n_blocks):
    f = pl.program_id(2)
    blk = pl.program_id(0) * pl.num_programs(1) + pl.program_id(1)
    step = blk * n_f + f
    per_step = tmr // n_f
    unroll = 8

    def row_copy(b, i):
        return pltpu.make_async_copy(src_ref.at[rows_ref[b * tmr + i]], gbuf.at[i], sem)

    def wait_block(b):
        def body(i, carry):
            for u in range(unroll):
                row_copy(b, i * unroll + u).wait()
            return carry
        lax.fori_loop(0, tmr // unroll, body, 0)

    @pl.when(step == 0)
    def _():
        acc_scr[...] = jnp.zeros(acc_scr.shape, F32)

        def body(i, carry):
            for u in range(unroll):
                row_copy(0, i * unroll + u).start()
            return carry
        lax.fori_loop(0, tmr // unroll, body, 0)

    @pl.when(f == 0)
    def _():
        wait_block(blk)
        hd = x_scr.shape[1] // 2
        tiles = pltpu.einshape("mkd->kmd", gbuf[...])
        for k in range(hd // LANES):
            words = tiles[k]
            x_scr[:, k * LANES:(k + 1) * LANES] = (
                pltpu.bitcast(words & jnp.uint32(0xFFFF0000), F32).astype(BF16))
            x_scr[:, hd + k * LANES:hd + (k + 1) * LANES] = (
                pltpu.bitcast(words << 16, F32).astype(BF16))

    nxt = jnp.minimum(blk + 1, n_blocks - 1)
    for i in range(per_step):
        row_copy(nxt, f * per_step + i).start()

    x = x_scr[...]
    g = jnp.dot(x, wg_ref[0, 0].astype(BF16), preferred_element_type=F32)
    u = jnp.dot(x, wu_ref[0, 0].astype(BF16), preferred_element_type=F32)
    hid = (g * _sigmoid(g) * u).astype(BF16)
    part = jnp.dot(hid, wd_ref[0, 0].astype(BF16), preferred_element_type=F32)
    acc = jnp.where(f == 0, part, acc_scr[...] + part)
    acc_scr[...] = acc
    o_ref[0] = acc * gate_ref[0]

    @pl.when(step == n_blocks * n_f - 1)
    def _():
        wait_block(nxt)


def _expert_ffn(rows_flat, packed, w_gate, w_up, w_down, gate_col, *, layer, rows, tmr, tf):
    ne, d, dff = w_gate.shape[1], w_gate.shape[2], w_gate.shape[3]
    n_r = rows // tmr
    n_f = dff // tf
    return pl.pallas_call(
        functools.partial(_ffn_kernel, tmr=tmr, n_f=n_f, n_blocks=ne * n_r),
        out_shape=jax.ShapeDtypeStruct((ne, rows, d), F32),
        grid_spec=pltpu.PrefetchScalarGridSpec(
            num_scalar_prefetch=1,
            grid=(ne, n_r, n_f),
            in_specs=[pl.BlockSpec(memory_space=pl.ANY),
                      pl.BlockSpec((1, 1, d, tf), lambda e, r, f, idx: (layer, e, 0, f)),
                      pl.BlockSpec((1, 1, d, tf), lambda e, r, f, idx: (layer, e, 0, f)),
                      pl.BlockSpec((1, 1, tf, d), lambda e, r, f, idx: (layer, e, f, 0)),
                      pl.BlockSpec((1, tmr, 1), lambda e, r, f, idx: (e, r, 0))],
            out_specs=pl.BlockSpec((1, tmr, d), lambda e, r, f, idx: (e, r, 0)),
            scratch_shapes=[pltpu.VMEM((tmr, d), F32),
                            pltpu.VMEM((tmr, d // 2 // LANES, LANES), U32),
                            pltpu.VMEM((tmr, d), BF16), pltpu.SemaphoreType.DMA(())],
        ),
        compiler_params=_cparams(("arbitrary", "arbitrary", "arbitrary")),
        name="ec_ffn",
    )(rows_flat, packed, w_gate, w_up, w_down, gate_col)


def _combine_kernel(cs_ref, nr_ref, pos_ref, x1_ref, fw_ref, y_hbm, o_ref, slab, sem, *,
                    w, cap, ne, nblk, final_norm):
    j = pl.program_id(1)
    step = pl.program_id(0) * nblk + j
    nsteps = pl.num_programs(0) * nblk
    cur = step % 2
    per = LANES // w

    def window(st, e, r):
        first = cs_ref[st * ne + e] + r * (w - SUBLANES)
        start = jnp.minimum((first // SUBLANES) * SUBLANES, cap - w)
        return first, pl.multiple_of(start, SUBLANES)

    def copy(st, e, r, buf):
        _, start = window(st, e, r)
        row = pl.multiple_of((st // nblk) * cap + start, SUBLANES)
        return pltpu.make_async_copy(y_hbm.at[e, pl.ds(row, w), :],
                                     slab.at[buf, pl.ds(e * w, w), :], sem.at[buf])

    def fetch(st, r, buf):
        for e in range(ne):
            copy(st, e, r, buf).start()

    def drain(st, r, buf):
        for e in range(ne):
            copy(st, e, r, buf).wait()

    @pl.when(step == 0)
    def _():
        fetch(0, 0, 0)

    drain(step, 0, cur)

    @pl.when(step + 1 < nsteps)
    def _():
        fetch(step + 1, 0, 1 - cur)

    pos = pos_ref[0]
    lane = lax.broadcasted_iota(I32, pos.shape, 1)
    lane_f = lane.astype(F32)

    def contrib(r, buf):
        cols = []
        for grp in range(ne // per):
            e = grp * per + per - 1
            first, start = window(step, e, r)
            pe = pos[:, e:e + 1]
            first_v = first.astype(F32)
            base_v = (start - (per - 1) * w).astype(F32)
            for q in range(per - 2, -1, -1):
                e = grp * per + q
                first, start = window(step, e, r)
                m = lane < (q + 1) * w
                pe = jnp.where(m, pos[:, e:e + 1], pe)
                first_v = jnp.where(m, first.astype(F32), first_v)
                base_v = jnp.where(m, (start - q * w).astype(F32), base_v)
            hit = jnp.logical_and(jnp.logical_and(pe >= first_v, pe < first_v + (w - SUBLANES)),
                                  pe - base_v == lane_f)
            cols.append(hit.astype(BF16))
        out = None
        for grp in range(0, len(cols), 2):
            sel = jnp.concatenate(cols[grp:grp + 2], axis=1)
            y = slab[buf, grp * LANES:(grp + 2) * LANES, :]
            y_hi = y.astype(BF16)
            y_lo = (y - y_hi.astype(F32)).astype(BF16)
            part = (jnp.dot(sel, y_hi, preferred_element_type=F32)
                    + jnp.dot(sel, y_lo, preferred_element_type=F32))
            out = part if out is None else out + part
        return out

    acc = x1_ref[...] + contrib(0, cur)

    def extra_round(r, acc):
        fetch(step, r, cur)
        drain(step, r, cur)
        return acc + contrib(r, cur)

    acc = lax.fori_loop(1, nr_ref[step], extra_round, acc)
    if final_norm:
        acc = acc * lax.rsqrt(jnp.mean(acc * acc, axis=-1, keepdims=True) + NORM_EPS) * fw_ref[...]
    o_ref[...] = acc


def _combine(cs_flat, nr_flat, post, x1, y, final_w, *, seq, tb, w, cap, final_norm):
    m, d = x1.shape
    ne = y.shape[0]
    nblk = seq // tb
    return pl.pallas_call(
        functools.partial(_combine_kernel, w=w, cap=cap, ne=ne, nblk=nblk, final_norm=final_norm),
        out_shape=jax.ShapeDtypeStruct((m, d), F32),
        grid_spec=pltpu.PrefetchScalarGridSpec(
            num_scalar_prefetch=2,
            grid=(m // seq, nblk),
            in_specs=[pl.BlockSpec((1, tb, LANES), lambda b, j, cs, nr: (b, j, 0)),
                      pl.BlockSpec((tb, d), lambda b, j, cs, nr: (b * nblk + j, 0)),
                      pl.BlockSpec((1, d), lambda b, j, cs, nr: (0, 0)),
                      pl.BlockSpec(memory_space=pl.ANY)],
            out_specs=pl.BlockSpec((tb, d), lambda b, j, cs, nr: (b * nblk + j, 0)),
            scratch_shapes=[pltpu.VMEM((2, ne * w, d), F32), pltpu.SemaphoreType.DMA((2,))],
        ),
        compiler_params=_cparams(("arbitrary", "arbitrary")),
        name="ec_combine",
    )(cs_flat, nr_flat, post, x1, final_w.reshape(1, d), y)


def _pick(n, pref):
    t = min(n, pref)
    while n % t:
        t //= 2
    return t


def kernel(x, norm1_w, w_in, conv_w, a_log, dt_bias, gdn_norm_w, diff_lambda, diff_subln_w, w_out,
           norm2_w, w_router, w_gate, w_up, w_down, final_norm_w):
    bsz, seq, d = x.shape
    depth = w_in.shape[0]
    d_gdn = d // 2
    d_diff = d - d_gdn
    n_gdn_heads = d_gdn // HEAD
    n_diff_heads = d_diff // (2 * HEAD)
    n_gate = N_DIR * n_gdn_heads
    n_experts = w_router.shape[2]
    cap = EC_CAPACITY * seq // n_experts
    m = bsz * seq
    c_gate = 4 * d_gdn
    c_diff = c_gate + 2 * n_gate

    half = HEAD // 2
    inv_freq = ROPE_THETA ** (-jnp.arange(half, dtype=F32) / half)
    ang = jnp.arange(seq, dtype=F32)[:, None] * inv_freq[None, :]
    cos_t = jnp.concatenate([jnp.cos(ang), jnp.cos(ang)], axis=1)
    sin_t = jnp.concatenate([-jnp.sin(ang), jnp.sin(ang)], axis=1)

    tm_proj = _pick(seq, 1024)
    x2 = x.reshape(m, d)
    for l in range(depth):
        w_l = w_in[l]
        w_gdn = w_l[:, :c_gate].astype(BF16)
        w_gt = jnp.zeros((d, LANES), BF16).at[:, :2 * n_gate].set(w_l[:, c_gate:c_diff].astype(BF16))
        w_df = w_l[:, c_diff:].astype(BF16)

        proj_gdn, gates = _norm_proj(x2, norm1_w[l], w_gdn, w_gt, tm=tm_proj, tn=PROJ_TN)
        n_blk = d_diff // PROJ_TN
        proj_diff = _norm_proj_rot(x2, norm1_w[l], w_df, cos_t, sin_t, seq=seq, tm=tm_proj,
                                   tn=PROJ_TN, n_q_blocks=n_blk, n_rot_blocks=2 * n_blk)

        qkv = _gdn_prep(proj_gdn.reshape(bsz, seq, 4 * d_gdn), conv_w[l], n_gdn_heads)
        gcol, grow = _gate_prep(gates, a_log[l], dt_bias[l], seq=seq, tm=_pick(seq, 512))
        o_f, o_b = _gdn_scan(qkv, gcol.reshape(bsz, seq, LANES), grow, n_gdn_heads, rows=_pick(seq, SCAN_ROWS))

        lf = diff_lambda[l].astype(F32)
        lambda_init = 0.8 - 0.6 * math.exp(-0.3 * l)
        lam = (jnp.exp(jnp.sum(lf[0] * lf[1])) - jnp.exp(jnp.sum(lf[2] * lf[3])) + lambda_init)
        attn = _diff_attn(proj_diff.reshape(bsz, seq, 3 * d_diff), lam.reshape(1).astype(F32),
                          n_diff_heads, tq=_pick(seq, ATTN_TQ))

        x1, n2, afft = _mix_out(
            o_f.reshape(m, d_gdn), o_b.reshape(m, d_gdn), proj_gdn, attn.reshape(m, d_diff), x2,
            gdn_norm_w[l], diff_subln_w[l], w_out[l].astype(BF16), norm2_w[l], w_router[l],
            seq=seq, tm=_pick(seq, 256), post_scale=1.0 - lambda_init)

        idx, gate, post, cs128 = _topk(afft, n_experts, cap)
        rows = (idx + (jnp.arange(bsz, dtype=I32) * seq)[:, None, None]).transpose(1, 0, 2)
        gate_col = gate.transpose(1, 0, 2).reshape(n_experts, bsz * cap, 1)
        y = _expert_ffn(rows.reshape(-1), n2, w_gate, w_up, w_down, gate_col, layer=l,
                        rows=bsz * cap, tmr=_pick(bsz * cap, FFN_TM), tf=FFN_TF)
        tb = _pick(seq, COMBINE_TB)
        win = min(COMBINE_W, cap)
        cs = cs128[:, :, ::tb // LANES].transpose(0, 2, 1)
        run = jnp.concatenate([cs[:, 1:], jnp.full((bsz, 1, n_experts), cap, I32)], axis=1) - cs
        per_round = win - SUBLANES
        rounds = jnp.maximum(1, jnp.max((run + per_round - 1) // per_round, axis=-1))
        x2 = _combine(cs.reshape(-1), rounds.reshape(-1).astype(I32), post, x1, y, final_norm_w,
                      seq=seq, tb=tb, w=win, cap=cap, final_norm=l == depth - 1)
    return x2.reshape(bsz, seq, d)
```

```python
import functools
import math

import jax
import jax.numpy as jnp
from jax import lax
from jax.experimental import pallas as pl
from jax.experimental.pallas import tpu as pltpu

F32 = jnp.float32
BF16 = jnp.bfloat16
I32 = jnp.int32
U32 = jnp.uint32

NORM_EPS = 1e-6
LANES = 128
SUBLANES = 8
HEAD = 128
GDN_CONV = 5
CHUNK = 64
N_DIR = 2
EC_CAPACITY = 2
ROPE_THETA = 10000.0
VMEM_LIMIT = 56 * 1024 * 1024
HI = lax.Precision.HIGHEST
SCAN_ROWS = 256
PREP_ROWS = 512
GATE_TM = 2048
ATTN_TQ = 256
ATTN_KC = 512
PROJ_TN = 1024
FFN_TM = 1024
FFN_TF = 256
COMBINE_TB = 256
COMBINE_W = 64


def _cparams(sem):
    return pltpu.CompilerParams(dimension_semantics=sem, vmem_limit_bytes=VMEM_LIMIT)


def _sigmoid(x):
    return 1.0 / (1.0 + jnp.exp(-x))


def _mm(a, b):
    return jnp.dot(a.astype(BF16), b.astype(BF16), preferred_element_type=F32)


def _mm_nt(a, b):
    return lax.dot_general(a.astype(BF16), b.astype(BF16), (((1,), (1,)), ((), ())),
                           preferred_element_type=F32)


def _mm_tn(a, b):
    return lax.dot_general(a.astype(BF16), b.astype(BF16), (((0,), (0,)), ((), ())),
                           preferred_element_type=F32)


def _norm_rows(x_ref, nw_ref, n_scr):
    x = x_ref[...]
    ms = jnp.mean(x * x, axis=-1, keepdims=True)
    n_scr[...] = (x * lax.rsqrt(ms + NORM_EPS) * nw_ref[...]).astype(BF16)


def _norm_proj_kernel(x_ref, nw_ref, w_ref, ws_ref, o_ref, os_ref, n_scr):
    @pl.when(pl.program_id(1) == 0)
    def _():
        _norm_rows(x_ref, nw_ref, n_scr)
        os_ref[...] = jnp.dot(n_scr[...], ws_ref[...], preferred_element_type=F32)

    o_ref[...] = jnp.dot(n_scr[...], w_ref[...], preferred_element_type=F32).astype(o_ref.dtype)


def _norm_proj_rot_kernel(x_ref, nw_ref, w_ref, cos_ref, sin_ref, o_ref, n_scr, *,
                          n_q_blocks, n_rot_blocks):
    j = pl.program_id(1)

    @pl.when(j == 0)
    def _():
        _norm_rows(x_ref, nw_ref, n_scr)

    acc = jnp.dot(n_scr[...], w_ref[...], preferred_element_type=F32)
    tn = acc.shape[1]
    is_rot = j < n_rot_blocks
    scale = jnp.where(j < n_q_blocks, HEAD ** -0.5, 1.0).astype(F32)
    c = cos_ref[...]
    s = sin_ref[...]
    for g in range(tn // HEAD):
        t = acc[:, g * HEAD:(g + 1) * HEAD]
        r = t * c + pltpu.roll(t, HEAD // 2, axis=1) * s
        o_ref[:, g * HEAD:(g + 1) * HEAD] = (jnp.where(is_rot, r, t) * scale).astype(o_ref.dtype)


def _norm_proj(x2d, nw, w_bf16, w_side, *, tm, tn):
    m, d = x2d.shape
    n = w_bf16.shape[1]
    ns = w_side.shape[1]
    return pl.pallas_call(
        _norm_proj_kernel,
        out_shape=(jax.ShapeDtypeStruct((m, n), F32), jax.ShapeDtypeStruct((m, ns), F32)),
        grid=(m // tm, n // tn),
        in_specs=[pl.BlockSpec((tm, d), lambda i, j: (i, 0)),
                  pl.BlockSpec((1, d), lambda i, j: (0, 0)),
                  pl.BlockSpec((d, tn), lambda i, j: (0, j)),
                  pl.BlockSpec((d, ns), lambda i, j: (0, 0))],
        out_specs=(pl.BlockSpec((tm, tn), lambda i, j: (i, j)),
                   pl.BlockSpec((tm, ns), lambda i, j: (i, 0))),
        scratch_shapes=[pltpu.VMEM((tm, d), BF16)],
        compiler_params=_cparams(("parallel", "arbitrary")),
        name="norm_proj",
    )(x2d, nw.reshape(1, d), w_bf16, w_side)


def _norm_proj_rot(x2d, nw, w_bf16, cos_t, sin_t, *, seq, tm, tn, n_q_blocks, n_rot_blocks):
    m, d = x2d.shape
    n = w_bf16.shape[1]
    nsb = seq // tm
    return pl.pallas_call(
        functools.partial(_norm_proj_rot_kernel, n_q_blocks=n_q_blocks, n_rot_blocks=n_rot_blocks),
        out_shape=jax.ShapeDtypeStruct((m, n), BF16),
        grid=(m // tm, n // tn),
        in_specs=[pl.BlockSpec((tm, d), lambda i, j: (i, 0)),
                  pl.BlockSpec((1, d), lambda i, j: (0, 0)),
                  pl.BlockSpec((d, tn), lambda i, j: (0, j)),
                  pl.BlockSpec((tm, HEAD), lambda i, j: (i % nsb, 0)),
                  pl.BlockSpec((tm, HEAD), lambda i, j: (i % nsb, 0))],
        out_specs=pl.BlockSpec((tm, tn), lambda i, j: (i, j)),
        scratch_shapes=[pltpu.VMEM((tm, d), BF16)],
        compiler_params=_cparams(("parallel", "arbitrary")),
        name="norm_proj_rot",
    )(x2d, nw.reshape(1, d), w_bf16, cos_t, sin_t)


def _gdn_prep_kernel(x_ref, w_ref, o_ref, *, n_heads):
    j = pl.program_id(1)
    w = w_ref[...]
    s = x_ref.shape[1]
    rows = min(s, PREP_ROWS)
    half = GDN_CONV // 2
    row = lax.broadcasted_iota(I32, (rows, HEAD), 0)
    is_q = j < n_heads
    is_qk = j < 2 * n_heads
    for r0 in range(0, s, rows):
        acc = x_ref[0, r0:r0 + rows, :] * w[half:half + 1, :]
        for k in range(GDN_CONV):
            off = k - half
            if off == 0:
                continue
            if 0 <= r0 + off and r0 + off + rows <= s:
                xs = x_ref[0, r0 + off:r0 + off + rows, :]
            else:
                xs = pltpu.roll(x_ref[0, r0:r0 + rows, :], (-off) % rows, axis=0)
                valid = jnp.logical_and(row + (r0 + off) >= 0, row + (r0 + off) < s)
                xs = jnp.where(valid, xs, 0.0)
            acc = acc + xs * w[k:k + 1, :]
        y = acc * _sigmoid(acc)
        ss = jnp.sum(y * y, axis=-1, keepdims=True)
        fac = jnp.where(is_qk, lax.rsqrt(ss + NORM_EPS) * jnp.where(is_q, HEAD ** -0.5, 1.0), 1.0)
        o_ref[0, r0:r0 + rows, :] = y * fac


def _gdn_prep(proj3d, conv_w, n_heads):
    b, s, _ = proj3d.shape
    ncol = 3 * n_heads
    return pl.pallas_call(
        functools.partial(_gdn_prep_kernel, n_heads=n_heads),
        out_shape=jax.ShapeDtypeStruct((b, s, ncol * HEAD), F32),
        grid=(b, ncol),
        in_specs=[pl.BlockSpec((1, s, HEAD), lambda i, j: (i, 0, j)),
                  pl.BlockSpec((GDN_CONV, HEAD), lambda i, j: (0, j))],
        out_specs=pl.BlockSpec((1, s, HEAD), lambda i, j: (i, 0, j)),
        compiler_params=_cparams(("parallel", "parallel")),
        name="gdn_prep",
    )(proj3d, conv_w)


def _gate_prep_kernel(x_ref, alog_ref, dtb_ref, col_ref, row_ref, *, n_gate):
    x = x_ref[...]
    lane = lax.broadcasted_iota(I32, x.shape, 1)
    beta = _sigmoid(x)
    a = x + dtb_ref[...]
    sp = jnp.maximum(a, 0.0) + jnp.log(1.0 + jnp.exp(-jnp.abs(a)))
    g = -jnp.exp(alog_ref[...]) * sp
    y = jnp.where(lane < n_gate, beta, jnp.where(lane < 2 * n_gate, g, 0.0))
    col_ref[...] = y
    row_ref[0] = y.T


def _gate_prep(gates2d, a_log, dt_bias, *, seq, tm):
    m = gates2d.shape[0]
    n_gate = a_log.size
    nsb = seq // tm
    pad = lambda v: jnp.zeros((1, LANES), F32).at[0, n_gate:2 * n_gate].set(v.reshape(-1).astype(F32))
    return pl.pallas_call(
        functools.partial(_gate_prep_kernel, n_gate=n_gate),
        out_shape=(jax.ShapeDtypeStruct((m, LANES), F32),
                   jax.ShapeDtypeStruct((m // seq, LANES, seq), F32)),
        grid=(m // tm,),
        in_specs=[pl.BlockSpec((tm, LANES), lambda i: (i, 0)),
                  pl.BlockSpec((1, LANES), lambda i: (0, 0)),
                  pl.BlockSpec((1, LANES), lambda i: (0, 0))],
        out_specs=(pl.BlockSpec((tm, LANES), lambda i: (i, 0)),
                   pl.BlockSpec((1, LANES, tm), lambda i: (i // nsb, 0, i % nsb))),
        compiler_params=_cparams(("parallel",)),
        name="gate_prep",
    )(gates2d, pad(a_log), pad(dt_bias))


def _gdn_scan_kernel(qkv_f_ref, qkv_b_ref, gcol_f_ref, gcol_b_ref, grow_f_ref, grow_b_ref,
                     of_ref, ob_ref, st_ref, *, n_heads, n_sub):
    @pl.when(pl.program_id(1) == 0)
    def _():
        st_ref[...] = jnp.zeros(st_ref.shape, F32)

    c = CHUNK
    ri = lax.broadcasted_iota(I32, (c, c), 0)
    ci = lax.broadcasted_iota(I32, (c, c), 1)
    eye = (ri == ci).astype(F32)
    strict = {False: ri > ci, True: ri < ci}
    incl = {False: ri >= ci, True: ri <= ci}
    lower_incl = (ri >= ci).astype(F32)
    upper_incl = (ri <= ci).astype(F32)
    ng = N_DIR * n_heads

    items = []
    for d in range(N_DIR):
        reverse = d == 1
        qkv_ref = qkv_b_ref if reverse else qkv_f_ref
        gcol_ref = gcol_b_ref if reverse else gcol_f_ref
        grow_ref = grow_b_ref if reverse else grow_f_ref
        for sub in range(n_sub):
            r0 = sub * c
            gcol = gcol_ref[0, r0:r0 + c, :]
            grow = grow_ref[0, ng:2 * ng, r0:r0 + c]
            gc_col = jnp.dot(upper_incl if reverse else lower_incl, gcol,
                             precision=HI, preferred_element_type=F32)
            gc_row = jnp.dot(grow, lower_incl if reverse else upper_incl,
                             precision=HI, preferred_element_type=F32)
            last = 0 if reverse else c - 1
            for h in range(n_heads):
                gi = d * n_heads + h
                it = dict(d=d, sub=sub, h=h, gi=gi, r0=r0, reverse=reverse)
                it["q"] = qkv_ref[0, r0:r0 + c, h * HEAD:(h + 1) * HEAD]
                it["k"] = qkv_ref[0, r0:r0 + c, (n_heads + h) * HEAD:(n_heads + h + 1) * HEAD]
                it["v"] = qkv_ref[0, r0:r0 + c, (2 * n_heads + h) * HEAD:(2 * n_heads + h + 1) * HEAD]
                it["beta"] = gcol[:, gi:gi + 1]
                it["gc_c"] = gc_col[:, ng + gi:ng + gi + 1]
                it["gc_r"] = gc_row[gi:gi + 1, :]
                it["gtot"] = it["gc_r"][:, last:last + 1]
                items.append(it)

    for it in items:
        m_incl = incl[it["reverse"]]
        gdiff = it["gc_c"] - it["gc_r"]
        it["decay"] = jnp.where(m_incl, jnp.exp(jnp.where(m_incl, gdiff, 0.0)), 0.0)
        it["kb"] = it["k"] * it["beta"]
    for it in items:
        it["aq"] = _mm_nt(jnp.concatenate([it["kb"], it["q"]], axis=0), it["k"])
    for it in items:
        low = jnp.where(strict[it["reverse"]], it["aq"][:c] * it["decay"], 0.0)
        it["qk"] = it["aq"][c:] * it["decay"]
        it["p"] = low
        it["inv"] = eye - low
    for _ in range(int(math.log2(c)) - 1):
        for it in items:
            it["p"] = _mm(it["p"], it["p"])
        for it in items:
            it["inv"] = it["inv"] + _mm(it["inv"], it["p"])
    for it in items:
        eg = jnp.exp(it["gc_c"])
        rhs = jnp.concatenate([it["v"] * it["beta"], it["kb"] * eg], axis=1)
        it["qg"] = it["q"] * eg
        it["kd"] = it["k"] * jnp.exp(it["gtot"] - it["gc_c"])
        it["sol"] = _mm(it["inv"], rhs)

    for step in range(n_sub):
        cur = [it for it in items if it["sub"] == (n_sub - 1 - step if it["reverse"] else step)]
        for it in cur:
            it["state"] = st_ref[it["gi"]]
            w = it["sol"][:, HEAD:]
            it["ws"] = _mm(jnp.concatenate([w, it["qg"]], axis=0), it["state"])
        for it in cur:
            it["v_new"] = it["sol"][:, :HEAD] - it["ws"][:c]
        for it in cur:
            o = it["ws"][c:] + _mm(it["qk"], it["v_new"])
            o_ref = ob_ref if it["reverse"] else of_ref
            o_ref[0, it["r0"]:it["r0"] + c, it["h"] * HEAD:(it["h"] + 1) * HEAD] = o
        for it in cur:
            st_ref[it["gi"]] = it["state"] * jnp.exp(it["gtot"]) + _mm_tn(it["kd"], it["v_new"])


def _gdn_scan(qkv, gcol, grow, n_heads, *, rows):
    b, s, _ = qkv.shape
    nblk = s // rows
    n_sub = rows // CHUNK
    dg = n_heads * HEAD
    fwd = lambda i, t: (i, t, 0)
    bwd = lambda i, t: (i, nblk - 1 - t, 0)
    out = jax.ShapeDtypeStruct((b, s, dg), F32)
    return pl.pallas_call(
        functools.partial(_gdn_scan_kernel, n_heads=n_heads, n_sub=n_sub),
        out_shape=(out, out),
        grid=(b, nblk),
        in_specs=[pl.BlockSpec((1, rows, 3 * dg), fwd),
                  pl.BlockSpec((1, rows, 3 * dg), bwd),
                  pl.BlockSpec((1, rows, LANES), fwd),
                  pl.BlockSpec((1, rows, LANES), bwd),
                  pl.BlockSpec((1, LANES, rows), lambda i, t: (i, 0, t)),
                  pl.BlockSpec((1, LANES, rows), lambda i, t: (i, 0, nblk - 1 - t))],
        out_specs=(pl.BlockSpec((1, rows, dg), fwd), pl.BlockSpec((1, rows, dg), bwd)),
        scratch_shapes=[pltpu.VMEM((N_DIR * n_heads, HEAD, HEAD), F32)],
        compiler_params=_cparams(("parallel", "arbitrary")),
        name="gdn_scan",
    )(qkv, qkv, gcol, gcol, grow, grow)


def _diff_attn_kernel(lam_ref, q_ref, k_ref, v_ref, o_ref, s_scr, m_scr, *, kc):
    @pl.when(pl.program_id(0) == 0)
    def _():
        s_scr[...] = jnp.zeros(s_scr.shape, F32)
        m_scr[...] = jnp.zeros(m_scr.shape, F32)

    for parity in range(2):
        @pl.when(pl.program_id(0) % 2 == parity)
        def _():
            _diff_attn_step(lam_ref, q_ref, k_ref, v_ref, o_ref, s_scr, m_scr,
                            cur=parity, prev=1 - parity, kc=kc)


def _diff_attn_step(lam_ref, q_ref, k_ref, v_ref, o_ref, s_scr, m_scr, *, cur, prev, kc):
    lam = lam_ref[0]
    q = q_ref[0]
    seq = k_ref.shape[1]
    nchunk = seq // kc
    halves = []
    for t in range(2):
        qt = q[:, t * HEAD:(t + 1) * HEAD]
        m_prev = m_scr[prev, t]
        m_lane = None
        l_lane = None
        acc = None
        for c in range(nchunk):
            kt = k_ref[0, c * kc:(c + 1) * kc, t * HEAD:(t + 1) * HEAD]
            s = lax.dot_general(qt, kt, (((1,), (1,)), ((), ())), preferred_element_type=F32)
            s_scr[cur, t, :, c * kc:(c + 1) * kc] = s
            for g in range(kc // LANES):
                sg = s[:, g * LANES:(g + 1) * LANES]
                m_lane = sg if m_lane is None else jnp.maximum(m_lane, sg)
            sp = s_scr[prev, t, :, c * kc:(c + 1) * kc]
            e = jnp.exp(sp - jnp.concatenate([m_prev] * (kc // LANES), axis=1))
            for g in range(kc // LANES):
                eg = e[:, g * LANES:(g + 1) * LANES]
                l_lane = eg if l_lane is None else l_lane + eg
            pv = jnp.dot(e.astype(BF16), v_ref[0, c * kc:(c + 1) * kc, :],
                         preferred_element_type=F32)
            acc = pv if acc is None else acc + pv
        m_scr[cur, t] = jnp.broadcast_to(jnp.max(m_lane, axis=-1, keepdims=True), m_lane.shape)
        halves.append(acc * (1.0 / jnp.sum(l_lane, axis=-1, keepdims=True)))
    o_ref[0] = halves[0] - lam * halves[1]


def _diff_attn(qkv3d, lam, n_heads, *, tq):
    b, s, _ = qkv3d.shape
    dv = 2 * HEAD
    nq = s // tq
    total = b * n_heads * nq

    def split(step):
        step = jnp.clip(step, 0, total - 1)
        return step // (n_heads * nq), (step // nq) % n_heads, step % nq

    def q_map(i):
        bi, h, t = split(i)
        return bi, t, h

    def k_map(i):
        bi, h, _ = split(i)
        return bi, 0, n_heads + h

    def v_map(i):
        bi, h, _ = split(i - 1)
        return bi, 0, 2 * n_heads + h

    def o_map(i):
        bi, h, t = split(i - 1)
        return bi, t, h

    return pl.pallas_call(
        functools.partial(_diff_attn_kernel, kc=_pick(s, ATTN_KC)),
        out_shape=jax.ShapeDtypeStruct((b, s, n_heads * dv), F32),
        grid_spec=pltpu.PrefetchScalarGridSpec(
            num_scalar_prefetch=0,
            grid=(total + 1,),
            in_specs=[pl.BlockSpec(memory_space=pltpu.SMEM),
                      pl.BlockSpec((1, tq, dv), q_map),
                      pl.BlockSpec((1, s, dv), k_map),
                      pl.BlockSpec((1, s, dv), v_map)],
            out_specs=pl.BlockSpec((1, tq, dv), o_map),
            scratch_shapes=[pltpu.VMEM((2, 2, tq, s), F32), pltpu.VMEM((2, 2, tq, LANES), F32)],
        ),
        compiler_params=_cparams(("arbitrary",)),
        name="diff_attn",
    )(lam, qkv3d, qkv3d, qkv3d)


def _mix_out_kernel(of_ref, ob_ref, z_ref, at_ref, x_ref, gw_ref, sw_ref, wo_ref, n2w_ref, wr_ref,
                    x1_ref, n2_ref, afft_ref, mix_scr, *, n_gdn_heads, n_diff_heads,
                    post_scale, n_experts):
    o = of_ref[...] + ob_ref[...]
    z = z_ref[...]
    gw = gw_ref[...]
    for h in range(n_gdn_heads):
        sl = slice(h * HEAD, (h + 1) * HEAD)
        seg = o[:, sl]
        y = seg * lax.rsqrt(jnp.mean(seg * seg, axis=-1, keepdims=True) + NORM_EPS) * gw
        zz = z[:, sl]
        mix_scr[:, sl] = (y * (zz * _sigmoid(zz))).astype(BF16)
    base = n_gdn_heads * HEAD
    at = at_ref[...]
    sw = sw_ref[...]
    dv = 2 * HEAD
    for h in range(n_diff_heads):
        seg = at[:, h * dv:(h + 1) * dv]
        y = seg * lax.rsqrt(jnp.mean(seg * seg, axis=-1, keepdims=True) + NORM_EPS) * sw
        mix_scr[:, base + h * dv:base + (h + 1) * dv] = (y * post_scale).astype(BF16)
    x1 = x_ref[...] + jnp.dot(mix_scr[...], wo_ref[...], preferred_element_type=F32)
    x1_ref[...] = x1
    n2 = x1 * lax.rsqrt(jnp.mean(x1 * x1, axis=-1, keepdims=True) + NORM_EPS) * n2w_ref[...]
    n_hi = n2.astype(BF16)
    hd = n2.shape[1] // 2
    bits = pltpu.bitcast(n_hi.astype(F32), U32)
    words = bits[:, :hd] | (bits[:, hd:] >> 16)
    chunks = jnp.stack([words[:, k * LANES:(k + 1) * LANES] for k in range(hd // LANES)], axis=0)
    n2_ref[...] = pltpu.einshape("kmd->mkd", chunks)
    wr = wr_ref[...]
    n_lo = (n2 - n_hi.astype(F32)).astype(BF16)
    w_hi = wr.astype(BF16)
    w_lo = (wr - w_hi.astype(F32)).astype(BF16)
    logits = (jnp.dot(n_hi, w_hi, preferred_element_type=F32)
              + jnp.dot(n_hi, w_lo, preferred_element_type=F32)
              + jnp.dot(n_lo, w_hi, preferred_element_type=F32))
    lane = lax.broadcasted_iota(I32, logits.shape, 1)
    logits = jnp.where(lane < n_experts, logits, -jnp.inf)
    mx = jnp.max(logits, axis=-1, keepdims=True)
    e = jnp.exp(logits - mx)
    aff = e / jnp.sum(e, axis=-1, keepdims=True)
    afft_ref[0] = aff.T


def _mix_out(o_f, o_b, proj_gdn, attn, x2d, gdn_norm_w, subln_w, w_out_bf16, norm2_w, w_router,
             *, seq, tm, post_scale):
    m, d = x2d.shape
    dg = o_f.shape[1]
    n_gdn_heads = dg // HEAD
    n_diff_heads = attn.shape[1] // (2 * HEAD)
    n_experts = w_router.shape[1]
    nsb = seq // tm
    gw = jnp.tile(gdn_norm_w.reshape(1, HEAD), (1, 1))
    sw = subln_w.reshape(1, 2 * HEAD)
    wr = jnp.zeros((d, LANES), F32).at[:, :n_experts].set(w_router)
    zblk = proj_gdn.shape[1] // dg - 1
    row = lambda i: (i, 0)
    const = lambda i: (0, 0)
    return pl.pallas_call(
        functools.partial(_mix_out_kernel, n_gdn_heads=n_gdn_heads, n_diff_heads=n_diff_heads,
                          post_scale=post_scale, n_experts=n_experts),
        out_shape=(jax.ShapeDtypeStruct((m, d), F32),
                   jax.ShapeDtypeStruct((m, d // 2 // LANES, LANES), U32),
                   jax.ShapeDtypeStruct((m // seq, LANES, seq), F32)),
        grid=(m // tm,),
        in_specs=[pl.BlockSpec((tm, dg), row), pl.BlockSpec((tm, dg), row),
                  pl.BlockSpec((tm, dg), lambda i: (i, zblk)),
                  pl.BlockSpec((tm, attn.shape[1]), row),
                  pl.BlockSpec((tm, d), row),
                  pl.BlockSpec((1, HEAD), const), pl.BlockSpec((1, 2 * HEAD), const),
                  pl.BlockSpec(w_out_bf16.shape, const),
                  pl.BlockSpec((1, d), const), pl.BlockSpec((d, LANES), const)],
        out_specs=(pl.BlockSpec((tm, d), row),
                   pl.BlockSpec((tm, d // 2 // LANES, LANES), lambda i: (i, 0, 0)),
                   pl.BlockSpec((1, LANES, tm), lambda i: (i // nsb, 0, i % nsb))),
        scratch_shapes=[pltpu.VMEM((tm, d), BF16)],
        compiler_params=_cparams(("parallel",)),
        name="mix_out",
    )(o_f, o_b, proj_gdn, attn, x2d, gw, sw, w_out_bf16, norm2_w.reshape(1, d), wr)


def _topk_kernel(afft_ref, idx_ref, gate_ref, post_ref, cs_ref, pos_scr, *, cap):
    x = afft_ref[0]
    ne, s = x.shape
    bits = pltpu.bitcast(x, I32)

    def bisect(i, prefix):
        cand = prefix | (jnp.int32(1) << (30 - i))
        cnt = jnp.sum((bits >= cand).astype(I32), axis=-1, keepdims=True)
        return jnp.where(cnt >= cap, cand, prefix)

    thr = lax.fori_loop(0, 31, bisect, jnp.zeros((ne, 1), I32))
    gt = bits > thr
    eq = bits == thr
    need = cap - jnp.sum(gt.astype(I32), axis=-1, keepdims=True)

    ri = lax.broadcasted_iota(I32, (LANES, LANES), 0)
    ci = lax.broadcasted_iota(I32, (LANES, LANES), 1)
    upper_strict = (ri < ci).astype(BF16)

    def excl_prefix(mask):
        carry = jnp.zeros((ne, 1), F32)
        parts, starts = [], []
        for blk in range(s // LANES):
            mb = mask[:, blk * LANES:(blk + 1) * LANES].astype(BF16)
            parts.append(jnp.dot(mb, upper_strict, preferred_element_type=F32) + carry)
            starts.append(carry)
            carry = carry + jnp.sum(mb.astype(F32), axis=-1, keepdims=True)
        return jnp.concatenate(parts, axis=1), jnp.concatenate(starts, axis=1)

    eq_rank, _ = excl_prefix(eq)
    sel = jnp.logical_or(gt, jnp.logical_and(eq, eq_rank < need.astype(F32)))
    pos, starts = excl_prefix(sel)

    posm = jnp.where(sel, pos, -1.0)
    pos_scr[...] = posm
    cs_ref[0] = starts.astype(I32)
    posm_pad = jnp.concatenate([posm, jnp.full((LANES - ne, s), -1.0, F32)], axis=0)
    for blk in range(s // LANES):
        post_ref[0, blk * LANES:(blk + 1) * LANES, :] = posm_pad[:, blk * LANES:(blk + 1) * LANES].T

    tb = min(s, 512)
    tok = lax.broadcasted_iota(I32, (1, tb), 1)
    slot = lax.broadcasted_iota(I32, (cap, tb), 0).astype(F32)
    zero = jnp.zeros((3, tb), F32)

    def per_expert(e, carry):
        r = jnp.zeros((8, cap), F32)
        for blk in range(s // tb):
            prow = pos_scr[pl.ds(e, 1), blk * tb:(blk + 1) * tb]
            onehot = (prow == slot).astype(BF16)
            a = afft_ref[0, pl.ds(e, 1), blk * tb:(blk + 1) * tb]
            a_hi = a.astype(BF16).astype(F32)
            a_mid = (a - a_hi).astype(BF16).astype(F32)
            a_lo = a - a_hi - a_mid
            t = tok + blk * tb
            lhs = jnp.concatenate([(t // 64).astype(F32), (t % 64).astype(F32),
                                   a_hi, a_mid, a_lo, zero], axis=0)
            r = r + _mm_nt(lhs, onehot)
        idx_ref[0, pl.ds(e, 1), :] = (r[0:1] * 64.0 + r[1:2]).astype(I32)
        gate_ref[0, pl.ds(e, 1), :] = r[2:3] + r[3:4] + r[4:5]
        return carry

    lax.fori_loop(0, ne, per_expert, 0)


def _topk(afft, n_experts, cap):
    b, _, s = afft.shape
    return pl.pallas_call(
        functools.partial(_topk_kernel, cap=cap),
        out_shape=(jax.ShapeDtypeStruct((b, n_experts, cap), I32),
                   jax.ShapeDtypeStruct((b, n_experts, cap), F32),
                   jax.ShapeDtypeStruct((b, s, LANES), F32),
                   jax.ShapeDtypeStruct((b, n_experts, s // LANES), I32)),
        grid=(b,),
        in_specs=[pl.BlockSpec((1, n_experts, s), lambda i: (i, 0, 0))],
        out_specs=(pl.BlockSpec((1, n_experts, cap), lambda i: (i, 0, 0)),
                   pl.BlockSpec((1, n_experts, cap), lambda i: (i, 0, 0)),
                   pl.BlockSpec((1, s, LANES), lambda i: (i, 0, 0)),
                   pl.BlockSpec((1, n_experts, s // LANES), lambda i: (i, 0, 0))),
        scratch_shapes=[pltpu.VMEM((n_experts, s), F32)],
        compiler_params=_cparams(("parallel",)),
        name="ec_topk",
    )(afft)


def _ffn_kernel(rows_ref, src_ref, wg_ref, wu_ref, wd_ref, gate_ref, o_ref,
                acc_scr, gbuf, x_scr, sem, *, tmr, n_f, n_blocks):
    f = pl.program_id(2)
    blk = pl.program_id(0) * pl.num_programs(1) + pl.program_id(1)
    step = blk * n_f + f
    per_step = tmr // n_f
    unroll = 8

    def row_copy(b, i):
        return pltpu.make_async_copy(src_ref.at[rows_ref[b * tmr + i]], gbuf.at[i], sem)

    def wait_block(b):
        def body(i, carry):
            for u in range(unroll):
                row_copy(b, i * unroll + u).wait()
            return carry
        lax.fori_loop(0, tmr // unroll, body, 0)

    @pl.when(step == 0)
    def _():
        acc_scr[...] = jnp.zeros(acc_scr.shape, F32)

        def body(i, carry):
            for u in range(unroll):
                row_copy(0, i * unroll + u).start()
            return carry
        lax.fori_loop(0, tmr // unroll, body, 0)

    @pl.when(f == 0)
    def _():
        wait_block(blk)
        hd = x_scr.shape[1] // 2
        tiles = pltpu.einshape("mkd->kmd", gbuf[...])
        for k in range(hd // LANES):
            words = tiles[k]
            x_scr[:, k * LANES:(k + 1) * LANES] = (
                pltpu.bitcast(words & jnp.uint32(0xFFFF0000), F32).astype(BF16))
            x_scr[:, hd + k * LANES:hd + (k + 1) * LANES] = (
                pltpu.bitcast(words << 16, F32).astype(BF16))

    nxt = jnp.minimum(blk + 1, n_blocks - 1)
    for i in range(per_step):
        row_copy(nxt, f * per_step + i).start()

    x = x_scr[...]
    g = jnp.dot(x, wg_ref[0, 0].astype(BF16), preferred_element_type=F32)
    u = jnp.dot(x, wu_ref[0, 0].astype(BF16), preferred_element_type=F32)
    hid = (g * _sigmoid(g) * u).astype(BF16)
    part = jnp.dot(hid, wd_ref[0, 0].astype(BF16), preferred_element_type=F32)
    acc = jnp.where(f == 0, part, acc_scr[...] + part)
    acc_scr[...] = acc
    o_ref[0] = acc * gate_ref[0]

    @pl.when(step == n_blocks * n_f - 1)
    def _():
        wait_block(nxt)


def _expert_ffn(rows_flat, packed, w_gate, w_up, w_down, gate_col, *, layer, rows, tmr, tf):
    ne, d, dff = w_gate.shape[1], w_gate.shape[2], w_gate.shape[3]
    n_r = rows // tmr
    n_f = dff // tf
    return pl.pallas_call(
        functools.partial(_ffn_kernel, tmr=tmr, n_f=n_f, n_blocks=ne * n_r),
        out_shape=jax.ShapeDtypeStruct((ne, rows, d), F32),
        grid_spec=pltpu.PrefetchScalarGridSpec(
            num_scalar_prefetch=1,
            grid=(ne, n_r, n_f),
            in_specs=[pl.BlockSpec(memory_space=pl.ANY),
                      pl.BlockSpec((1, 1, d, tf), lambda e, r, f, idx: (layer, e, 0, f)),
                      pl.BlockSpec((1, 1, d, tf), lambda e, r, f, idx: (layer, e, 0, f)),
                      pl.BlockSpec((1, 1, tf, d), lambda e, r, f, idx: (layer, e, f, 0)),
                      pl.BlockSpec((1, tmr, 1), lambda e, r, f, idx: (e, r, 0))],
            out_specs=pl.BlockSpec((1, tmr, d), lambda e, r, f, idx: (e, r, 0)),
            scratch_shapes=[pltpu.VMEM((tmr, d), F32),
                            pltpu.VMEM((tmr, d // 2 // LANES, LANES), U32),
                            pltpu.VMEM((tmr, d), BF16), pltpu.SemaphoreType.DMA(())],
        ),
        compiler_params=_cparams(("arbitrary", "arbitrary", "arbitrary")),
        name="ec_ffn",
    )(rows_flat, packed, w_gate, w_up, w_down, gate_col)


def _combine_kernel(cs_ref, nr_ref, pos_ref, x1_ref, fw_ref, y_hbm, o_ref, slab, sem, *,
                    w, cap, ne, nblk, final_norm):
    j = pl.program_id(1)
    step = pl.program_id(0) * nblk + j
    nsteps = pl.num_programs(0) * nblk
    cur = step % 2
    per = LANES // w

    def window(st, e, r):
        first = cs_ref[st * ne + e] + r * (w - SUBLANES)
        start = jnp.minimum((first // SUBLANES) * SUBLANES, cap - w)
        return first, pl.multiple_of(start, SUBLANES)

    def copy(st, e, r, buf):
        _, start = window(st, e, r)
        row = pl.multiple_of((st // nblk) * cap + start, SUBLANES)
        return pltpu.make_async_copy(y_hbm.at[e, pl.ds(row, w), :],
                                     slab.at[buf, pl.ds(e * w, w), :], sem.at[buf])

    def fetch(st, r, buf):
        for e in range(ne):
            copy(st, e, r, buf).start()

    def drain(st, r, buf):
        for e in range(ne):
            copy(st, e, r, buf).wait()

    @pl.when(step == 0)
    def _():
        fetch(0, 0, 0)

    drain(step, 0, cur)

    @pl.when(step + 1 < nsteps)
    def _():
        fetch(step + 1, 0, 1 - cur)

    pos = pos_ref[0]
    lane = lax.broadcasted_iota(I32, pos.shape, 1)
    lane_f = lane.astype(F32)

    def contrib(r, buf):
        cols = []
        for grp in range(ne // per):
            e = grp * per + per - 1
            first, start = window(step, e, r)
            pe = pos[:, e:e + 1]
            first_v = first.astype(F32)
            base_v = (start - (per - 1) * w).astype(F32)
            for q in range(per - 2, -1, -1):
                e = grp * per + q
                first, start = window(step, e, r)
                m = lane < (q + 1) * w
                pe = jnp.where(m, pos[:, e:e + 1], pe)
                first_v = jnp.where(m, first.astype(F32), first_v)
                base_v = jnp.where(m, (start - q * w).astype(F32), base_v)
            hit = jnp.logical_and(jnp.logical_and(pe >= first_v, pe < first_v + (w - SUBLANES)),
                                  pe - base_v == lane_f)
            cols.append(hit.astype(BF16))
        out = None
        for grp in range(0, len(cols), 2):
            sel = jnp.concatenate(cols[grp:grp + 2], axis=1)
            y = slab[buf, grp * LANES:(grp + 2) * LANES, :]
            y_hi = y.astype(BF16)
            y_lo = (y - y_hi.astype(F32)).astype(BF16)
            part = (jnp.dot(sel, y_hi, preferred_element_type=F32)
                    + jnp.dot(sel, y_lo, preferred_element_type=F32))
            out = part if out is None else out + part
        return out

    acc = x1_ref[...] + contrib(0, cur)

    def extra_round(r, acc):
        fetch(step, r, cur)
        drain(step, r, cur)
        return acc + contrib(r, cur)

    acc = lax.fori_loop(1, nr_ref[step], extra_round, acc)
    if final_norm:
        acc = acc * lax.rsqrt(jnp.mean(acc * acc, axis=-1, keepdims=True) + NORM_EPS) * fw_ref[...]
    o_ref[...] = acc


def _combine(cs_flat, nr_flat, post, x1, y, final_w, *, seq, tb, w, cap, final_norm):
    m, d = x1.shape
    ne = y.shape[0]
    nblk = seq // tb
    return pl.pallas_call(
        functools.partial(_combine_kernel, w=w, cap=cap, ne=ne, nblk=nblk, final_norm=final_norm),
        out_shape=jax.ShapeDtypeStruct((m, d), F32),
        grid_spec=pltpu.PrefetchScalarGridSpec(
            num_scalar_prefetch=2,
            grid=(m // seq, nblk),
            in_specs=[pl.BlockSpec((1, tb, LANES), lambda b, j, cs, nr: (b, j, 0)),
                      pl.BlockSpec((tb, d), lambda b, j, cs, nr: (b * nblk + j, 0)),
                      pl.BlockSpec((1, d), lambda b, j, cs, nr: (0, 0)),
                      pl.BlockSpec(memory_space=pl.ANY)],
            out_specs=pl.BlockSpec((tb, d), lambda b, j, cs, nr: (b * nblk + j, 0)),
            scratch_shapes=[pltpu.VMEM((2, ne * w, d), F32), pltpu.SemaphoreType.DMA((2,))],
        ),
        compiler_params=_cparams(("arbitrary", "arbitrary")),
        name="ec_combine",
    )(cs_flat, nr_flat, post, x1, final_w.reshape(1, d), y)


def _pick(n, pref):
    t = min(n, pref)
    while n % t:
        t //= 2
    return t


def kernel(x, norm1_w, w_in, conv_w, a_log, dt_bias, gdn_norm_w, diff_lambda, diff_subln_w, w_out,
           norm2_w, w_router, w_gate, w_up, w_down, final_norm_w):
    bsz, seq, d = x.shape
    depth = w_in.shape[0]
    d_gdn = d // 2
    d_diff = d - d_gdn
    n_gdn_heads = d_gdn // HEAD
    n_diff_heads = d_diff // (2 * HEAD)
    n_gate = N_DIR * n_gdn_heads
    n_experts = w_router.shape[2]
    cap = EC_CAPACITY * seq // n_experts
    m = bsz * seq
    c_gate = 4 * d_gdn
    c_diff = c_gate + 2 * n_gate

    half = HEAD // 2
    inv_freq = ROPE_THETA ** (-jnp.arange(half, dtype=F32) / half)
    ang = jnp.arange(seq, dtype=F32)[:, None] * inv_freq[None, :]
    cos_t = jnp.concatenate([jnp.cos(ang), jnp.cos(ang)], axis=1)
    sin_t = jnp.concatenate([-jnp.sin(ang), jnp.sin(ang)], axis=1)

    tm_proj = _pick(seq, 1024)
    x2 = x.reshape(m, d)
    for l in range(depth):
        w_l = w_in[l]
        w_gdn = w_l[:, :c_gate].astype(BF16)
        w_gt = jnp.zeros((d, LANES), BF16).at[:, :2 * n_gate].set(w_l[:, c_gate:c_diff].astype(BF16))
        w_df = w_l[:, c_diff:].astype(BF16)

        proj_gdn, gates = _norm_proj(x2, norm1_w[l], w_gdn, w_gt, tm=tm_proj, tn=PROJ_TN)
        n_blk = d_diff // PROJ_TN
        proj_diff = _norm_proj_rot(x2, norm1_w[l], w_df, cos_t, sin_t, seq=seq, tm=tm_proj,
                                   tn=PROJ_TN, n_q_blocks=n_blk, n_rot_blocks=2 * n_blk)

        qkv = _gdn_prep(proj_gdn.reshape(bsz, seq, 4 * d_gdn), conv_w[l], n_gdn_heads)
        gcol, grow = _gate_prep(gates, a_log[l], dt_bias[l], seq=seq, tm=_pick(seq, GATE_TM))
        o_f, o_b = _gdn_scan(qkv, gcol.reshape(bsz, seq, LANES), grow, n_gdn_heads, rows=_pick(seq, SCAN_ROWS))

        lf = diff_lambda[l].astype(F32)
        lambda_init = 0.8 - 0.6 * math.exp(-0.3 * l)
        lam = (jnp.exp(jnp.sum(lf[0] * lf[1])) - jnp.exp(jnp.sum(lf[2] * lf[3])) + lambda_init)
        attn = _diff_attn(proj_diff.reshape(bsz, seq, 3 * d_diff), lam.reshape(1).astype(F32),
                          n_diff_heads, tq=_pick(seq, ATTN_TQ))

        x1, n2, afft = _mix_out(
            o_f.reshape(m, d_gdn), o_b.reshape(m, d_gdn), proj_gdn, attn.reshape(m, d_diff), x2,
            gdn_norm_w[l], diff_subln_w[l], w_out[l].astype(BF16), norm2_w[l], w_router[l],
            seq=seq, tm=_pick(seq, 256), post_scale=1.0 - lambda_init)

        idx, gate, post, cs128 = _topk(afft, n_experts, cap)
        rows = (idx + (jnp.arange(bsz, dtype=I32) * seq)[:, None, None]).transpose(1, 0, 2)
        gate_col = gate.transpose(1, 0, 2).reshape(n_experts, bsz * cap, 1)
        y = _expert_ffn(rows.reshape(-1), n2, w_gate, w_up, w_down, gate_col, layer=l,
                        rows=bsz * cap, tmr=_pick(bsz * cap, FFN_TM), tf=FFN_TF)
        tb = _pick(seq, COMBINE_TB)
        win = min(COMBINE_W, cap)
        cs = cs128[:, :, ::tb // LANES].transpose(0, 2, 1)
        run = jnp.concatenate([cs[:, 1:], jnp.full((bsz, 1, n_experts), cap, I32)], axis=1) - cs
        per_round = win - SUBLANES
        rounds = jnp.maximum(1, jnp.max((run + per_round - 1) // per_round, axis=-1))
        x2 = _combine(cs.reshape(-1), rounds.reshape(-1).astype(I32), post, x1, y, final_norm_w,
                      seq=seq, tb=tb, w=win, cap=cap, final_norm=l == depth - 1)
    return x2.reshape(bsz, seq, d)
```
